```python
import jax, jax.numpy as jnp
from jax import lax
import numpy as np

D_MODEL = 1024
BATCH = 8
SEQ = 2048
DEPTH = 1

D_FF = 2816
POOL_WIDTH = D_MODEL // 2
POOL_WINDOWS = (2, 4, 8, 16)
N_POOL_GROUPS = len(POOL_WINDOWS)
POOL_GROUP = POOL_WIDTH // N_POOL_GROUPS
N_SB_HEADS = 8
SB_HEAD_DIM = 64
SB_WIDTH = N_SB_HEADS * SB_HEAD_DIM
Q_BLOCK = 128
IN_WIDTH = POOL_WIDTH + 3 * SB_WIDTH + 2 * D_MODEL
RMS_EPS = 1e-6

kernel_name = "macaron_pool_stickbreaking_gated_hybrid"


def rmsnorm(x, g):
    xf = x.astype(jnp.float32)
    r = lax.rsqrt(jnp.mean(xf * xf, axis=-1, keepdims=True) + RMS_EPS)
    return (xf * r * g.astype(jnp.float32)).astype(x.dtype)


def swiglu(x, w_gate_up, w_down):
    gu = x @ w_gate_up
    g, u = jnp.split(gu, 2, axis=-1)
    return (jax.nn.silu(g) * u) @ w_down


def causal_pool_mixer(xp, w_group, pool_scale):
    b, s, _ = xp.shape
    xg = xp.reshape(b, s, N_POOL_GROUPS, POOL_GROUP)
    pos = jnp.arange(s, dtype=jnp.int32)
    outs = []
    for gi, w in enumerate(POOL_WINDOWS):
        xi = xg[:, :, gi, :].astype(jnp.float32)
        cs = jnp.cumsum(xi, axis=1)
        lower = jnp.concatenate([jnp.zeros((b, w, POOL_GROUP), cs.dtype), cs[:, : s - w]], axis=1)
        count = jnp.minimum(pos + 1, w).astype(jnp.float32)[None, :, None]
        mean = (cs - lower) / count
        outs.append((mean - xi).astype(xp.dtype))
    y = jnp.stack(outs, axis=2)
    y = jnp.einsum('bsgc,gcd->bsgd', y, w_group)
    return y.reshape(b, s, POOL_WIDTH) * pool_scale


def stick_breaking_attention(q, k, v):
    s_len = q.shape[2]
    scale = 1.0 / np.sqrt(SB_HEAD_DIM)
    outs = []
    for blk in range(s_len // Q_BLOCK):
        i0, i1 = blk * Q_BLOCK, (blk + 1) * Q_BLOCK
        qb = q[:, :, i0:i1]
        kb = k[:, :, :i1]
        vb = v[:, :, :i1]
        z = jnp.einsum('bhqd,bhkd->bhqk', qb, kb).astype(jnp.float32) * scale
        qpos = jnp.arange(i0, i1)[:, None]
        kpos = jnp.arange(i1)[None, :]
        mask = kpos < qpos
        log_beta = jax.nn.log_sigmoid(z)
        log_1m_beta = jnp.where(mask, jax.nn.log_sigmoid(-z), 0.0)
        log_a = log_beta + lax.cumsum(log_1m_beta, axis=3, reverse=True) - log_1m_beta
        a = jnp.where(mask, jnp.exp(log_a), 0.0)
        outs.append(jnp.einsum('bhqk,bhkd->bhqd', a.astype(vb.dtype), vb))
    return jnp.concatenate(outs, axis=2)


def gated_mixer_block(u, w_in, pool_w_group, pool_scale, w_branch_pool, w_branch_attn, w_out):
    b, s, _ = u.shape
    proj = u @ w_in
    o1 = POOL_WIDTH
    o2 = o1 + SB_WIDTH
    o3 = o2 + SB_WIDTH
    o4 = o3 + SB_WIDTH
    xp = proj[..., :o1]
    q = proj[..., o1:o2].reshape(b, s, N_SB_HEADS, SB_HEAD_DIM).transpose(0, 2, 1, 3)
    k = proj[..., o2:o3].reshape(b, s, N_SB_HEADS, SB_HEAD_DIM).transpose(0, 2, 1, 3)
    v = proj[..., o3:o4].reshape(b, s, N_SB_HEADS, SB_HEAD_DIM).transpose(0, 2, 1, 3)
    gate_logits = proj[..., o4:]
    y_pool = causal_pool_mixer(xp, pool_w_group, pool_scale) @ w_branch_pool
    o_sb = stick_breaking_attention(q, k, v).transpose(0, 2, 1, 3).reshape(b, s, SB_WIDTH)
    y_sb = o_sb @ w_branch_attn
    g = jax.nn.sigmoid(gate_logits.astype(jnp.float32)).astype(u.dtype)
    g_pool, g_sb = jnp.split(g, 2, axis=-1)
    return (g_pool * y_pool + g_sb * y_sb) @ w_out


def setup_inputs(seed: int = 0) -> dict:
    key = jax.random.key(seed)
    ks = jax.random.split(key, 20)
    f32 = jnp.float32

    def w(k, shape, fan_in):
        return jax.random.normal(k, shape, f32) * (fan_in ** -0.5)

    def gain(k, shape):
        return 1.0 + 0.05 * jax.random.normal(k, shape, f32)

    L = DEPTH
    return {
        "x": jax.random.normal(ks[0], (BATCH, SEQ, D_MODEL), f32),
        "ffn1_norm": gain(ks[1], (L, D_MODEL)),
        "ffn1_w_gate_up": w(ks[2], (L, D_MODEL, 2 * D_FF), D_MODEL),
        "ffn1_w_down": w(ks[3], (L, D_FF, D_MODEL), D_FF),
        "mix_norm": gain(ks[4], (L, D_MODEL)),
        "w_in": w(ks[5], (L, D_MODEL, IN_WIDTH), D_MODEL),
        "pool_w_group": w(ks[6], (L, N_POOL_GROUPS, POOL_GROUP, POOL_GROUP), POOL_GROUP),
        "pool_scale": gain(ks[7], (L, POOL_WIDTH)),
        "w_branch_pool": w(ks[8], (L, POOL_WIDTH, D_MODEL), POOL_WIDTH),
        "w_branch_attn": w(ks[9], (L, SB_WIDTH, D_MODEL), SB_WIDTH),
        "w_out": w(ks[10], (L, D_MODEL, D_MODEL), D_MODEL),
        "ffn2_norm": gain(ks[11], (L, D_MODEL)),
        "ffn2_w_gate_up": w(ks[12], (L, D_MODEL, 2 * D_FF), D_MODEL),
        "ffn2_w_down": w(ks[13], (L, D_FF, D_MODEL), D_FF),
        "final_norm": gain(ks[14], (D_MODEL,)),
    }


def reference(x, ffn1_norm, ffn1_w_gate_up, ffn1_w_down, mix_norm, w_in, pool_w_group,
              pool_scale, w_branch_pool, w_branch_attn, w_out, ffn2_norm, ffn2_w_gate_up,
              ffn2_w_down, final_norm):
    h = x
    for l in range(DEPTH):
        h = h + 0.5 * swiglu(rmsnorm(h, ffn1_norm[l]), ffn1_w_gate_up[l], ffn1_w_down[l])
        u = rmsnorm(h, mix_norm[l])
        h = h + gated_mixer_block(u, w_in[l], pool_w_group[l], pool_scale[l],
                                  w_branch_pool[l], w_branch_attn[l], w_out[l])
        h = h + 0.5 * swiglu(rmsnorm(h, ffn2_norm[l]), ffn2_w_gate_up[l], ffn2_w_down[l])
    return rmsnorm(h, final_norm)
```

```python
import functools

import jax
import jax.numpy as jnp
from jax import lax
from jax.experimental import pallas as pl
from jax.experimental.pallas import tpu as pltpu

F32 = jnp.float32
BF16 = jnp.bfloat16

RMS_EPS = 1e-6
POOL_WINDOWS = (2, 4, 8, 16)
N_SB_HEADS = 8
SB_HEAD_DIM = 64
LANES = 128
HALO = 16
TOKEN_TILE = 256
FF_CHUNK = 256
VMEM_LIMIT_BYTES = 56 * 1024 * 1024


def _rmsnorm(x, g_row):
    ms = jnp.mean(x * x, axis=-1, keepdims=True)
    return x * lax.rsqrt(ms + RMS_EPS) * g_row


def _swiglu(xn, wgu_ref, wd_ref, act_ref):
    d_ff = wd_ref.shape[0]
    for c0 in range(0, d_ff, FF_CHUNK):
        g = jnp.dot(xn, wgu_ref[:, c0:c0 + FF_CHUNK], preferred_element_type=F32)
        u = jnp.dot(xn, wgu_ref[:, d_ff + c0:d_ff + c0 + FF_CHUNK], preferred_element_type=F32)
        act_ref[:, c0:c0 + FF_CHUNK] = (g * jax.nn.sigmoid(g) * u).astype(BF16)
    return jnp.dot(act_ref[...], wd_ref[...], preferred_element_type=F32)


def _ffn_proj_kernel(x_ref, n1_ref, wgu_ref, wd_ref, nm_ref, win_ref,
                     h1_ref, xp_ref, q_ref, k_ref, v_ref, gl_ref, act_ref, *, pool_width, sb_width):
    x = x_ref[...]
    xn = _rmsnorm(x, n1_ref[...]).astype(BF16)
    h1 = x + 0.5 * _swiglu(xn, wgu_ref, wd_ref, act_ref)
    h1_ref[...] = h1
    un = _rmsnorm(h1, nm_ref[...]).astype(BF16)

    o1 = pool_width
    xp_ref[...] = jnp.dot(un, win_ref[:, 0:o1], preferred_element_type=F32)
    n_pairs = sb_width // LANES
    scale = 1.0 / (SB_HEAD_DIM ** 0.5)
    for idx, (ref, s) in enumerate(((q_ref, scale), (k_ref, None), (v_ref, None))):
        lo = o1 + idx * sb_width
        t = jnp.dot(un, win_ref[:, lo:lo + sb_width], preferred_element_type=F32)
        if s is not None:
            t = t * s
        for p in range(n_pairs):
            ref[p] = t[:, p * LANES:(p + 1) * LANES].astype(BF16)
    o4 = o1 + 3 * sb_width
    gw = gl_ref.shape[1]
    for c0 in range(0, gw, 512):
        gl_ref[:, c0:c0 + 512] = jnp.dot(un, win_ref[:, o4 + c0:o4 + c0 + 512], preferred_element_type=F32)


def _attend_pair(qp, k_ref, v_ref, p, i, tri, diag_mask, lane_lo):
    tq = qp.shape[0]
    tk = tq
    zero = jnp.zeros_like(qp)
    qh = (jnp.where(lane_lo, qp, zero), jnp.where(lane_lo, zero, qp))

    def block(j, carry, masked):
        o, cs = carry
        r0 = pl.multiple_of(j * tk, tk)
        kb = k_ref[p, pl.ds(r0, tk), :]
        vb = v_ref[p, pl.ds(r0, tk), :]
        vzero = jnp.zeros_like(vb)
        vh = (jnp.where(lane_lo, vb, vzero), jnp.where(lane_lo, vzero, vb))
        new_cs = []
        for h in range(2):
            z = lax.dot_general(qh[h], kb, (((1,), (1,)), ((), ())), preferred_element_type=F32)
            l = jnp.minimum(-z, 0.0) - jnp.log(1.0 + jnp.exp(-jnp.abs(z)))
            log_beta = z + l
            if masked:
                l = jnp.where(diag_mask, l, 0.0)
            hi = l.astype(BF16)
            lo = (l - hi.astype(F32)).astype(BF16)
            cum = (jnp.dot(hi, tri, preferred_element_type=F32)
                   + jnp.dot(lo, tri, preferred_element_type=F32))
            a = jnp.exp(log_beta + cum + cs[h])
            if masked:
                a = jnp.where(diag_mask, a, 0.0)
            o = o + jnp.dot(a.astype(BF16), vh[h], preferred_element_type=F32)
            new_cs.append(cs[h] + jnp.sum(l, axis=-1, keepdims=True))
        return o, tuple(new_cs)

    init = (jnp.zeros((tq, LANES), F32), (jnp.zeros((tq, 1), F32), jnp.zeros((tq, 1), F32)))
    carry = block(i, init, True)
    o, _ = lax.fori_loop(0, i, lambda t, c: block(i - 1 - t, c, False), carry)
    return o


def _mixer_ffn_kernel(h1_ref, q_ref, k_ref, v_ref, xp_ref, xph_ref, gl_ref,
                      pw_ref, ps_ref, wbp_ref, wba_ref, wo_ref, n2_ref, wgu_ref, wd_ref, nf_ref,
                      out_ref, osb_ref, pm_ref, act_ref):
    i = pl.program_id(1)
    tq = h1_ref.shape[0]
    d_model = h1_ref.shape[1]

    row = lax.broadcasted_iota(jnp.int32, (tq, tq), 0)
    col = lax.broadcasted_iota(jnp.int32, (tq, tq), 1)
    tri = (row > col).astype(BF16)
    diag_mask = col < row
    lane_lo = lax.broadcasted_iota(jnp.int32, (1, LANES), 1) < SB_HEAD_DIM
    for p in range(q_ref.shape[0]):
        osb_ref[:, p * LANES:(p + 1) * LANES] = _attend_pair(
            q_ref[p], k_ref, v_ref, p, i, tri, diag_mask, lane_lo)

    halo = xph_ref[...] * (i > 0).astype(F32)
    pos = i * tq + lax.broadcasted_iota(jnp.int32, (tq, 1), 0)
    group = pw_ref.shape[1]
    for gi, w in enumerate(POOL_WINDOWS):
        lanes = slice(gi * group, (gi + 1) * group)
        xg = xp_ref[:, lanes]
        s = jnp.concatenate([halo[:, lanes], xg], axis=0)
        d = 1
        while d < w:
            s = s + pltpu.roll(s, d, 0)
            d *= 2
        count = jnp.minimum(pos + 1, w).astype(F32)
        y = s[HALO:, :] / count - xg
        yg = jnp.dot(y.astype(BF16), pw_ref[gi], preferred_element_type=F32)
        pm_ref[:, lanes] = (yg * ps_ref[:, lanes]).astype(BF16)

    y_pool = jnp.dot(pm_ref[...], wbp_ref[...], preferred_element_type=F32)
    y_sb = jnp.dot(osb_ref[...].astype(BF16), wba_ref[...], preferred_element_type=F32)
    g_pool = jax.nn.sigmoid(gl_ref[:, 0:d_model])
    g_sb = jax.nn.sigmoid(gl_ref[:, d_model:2 * d_model])
    m = (g_pool * y_pool + g_sb * y_sb).astype(BF16)
    h2 = h1_ref[...] + jnp.dot(m, wo_ref[...], preferred_element_type=F32)

    xn = _rmsnorm(h2, n2_ref[...]).astype(BF16)
    h3 = h2 + 0.5 * _swiglu(xn, wgu_ref, wd_ref, act_ref)
    out_ref[...] = _rmsnorm(h3, nf_ref[...])


def _resident(shape):
    nd = len(shape)
    return pl.BlockSpec(shape, lambda *_: (0,) * nd, pipeline_mode=pl.Buffered(1))


def _layer(x2, batch, seq, n1, wgu1, wd1, nm, win, pw, ps, wbp, wba, wo, n2, wgu2, wd2, nf):
    tokens, d_model = x2.shape
    d_ff = wd1.shape[0]
    pool_width = wbp.shape[0]
    sb_width = wba.shape[0]
    n_pairs = sb_width // LANES
    gate_width = 2 * d_model
    tm = TOKEN_TILE
    assert seq % tm == 0 and tm % HALO == 0 and d_ff % FF_CHUNK == 0 and gate_width % 512 == 0
    assert sb_width == N_SB_HEADS * SB_HEAD_DIM and max(POOL_WINDOWS) - 1 <= HALO
    n_tiles = tokens // tm
    tiles_per_seq = seq // tm
    params = functools.partial(pltpu.CompilerParams, vmem_limit_bytes=VMEM_LIMIT_BYTES)

    row_tile = lambda width: pl.BlockSpec((tm, width), lambda t: (t, 0))
    pair_tile = pl.BlockSpec((n_pairs, tm, LANES), lambda t: (0, t, 0))
    h1, xp, q, k, v, gl = pl.pallas_call(
        functools.partial(_ffn_proj_kernel, pool_width=pool_width, sb_width=sb_width),
        grid=(n_tiles,),
        in_specs=[row_tile(d_model), _resident(n1.shape), _resident(wgu1.shape), _resident(wd1.shape),
                  _resident(nm.shape), _resident(win.shape)],
        out_specs=[row_tile(d_model), row_tile(pool_width), pair_tile, pair_tile, pair_tile,
                   row_tile(gate_width)],
        out_shape=[jax.ShapeDtypeStruct((tokens, d_model), F32),
                   jax.ShapeDtypeStruct((tokens, pool_width), F32),
                   jax.ShapeDtypeStruct((n_pairs, tokens, LANES), BF16),
                   jax.ShapeDtypeStruct((n_pairs, tokens, LANES), BF16),
                   jax.ShapeDtypeStruct((n_pairs, tokens, LANES), BF16),
                   jax.ShapeDtypeStruct((tokens, gate_width), F32)],
        scratch_shapes=[pltpu.VMEM((tm, d_ff), BF16)],
        compiler_params=params(dimension_semantics=("arbitrary",)),
        name="ffn1_proj",
    )(x2, n1, wgu1, wd1, nm, win)

    q_tile = lambda width: pl.BlockSpec((tm, width), lambda b, i: (b * tiles_per_seq + i, 0))
    halo_blocks = tm // HALO
    out = pl.pallas_call(
        _mixer_ffn_kernel,
        grid=(batch, tiles_per_seq),
        in_specs=[q_tile(d_model),
                  pl.BlockSpec((n_pairs, tm, LANES), lambda b, i: (0, b * tiles_per_seq + i, 0)),
                  pl.BlockSpec((n_pairs, seq, LANES), lambda b, i: (0, b, 0), pipeline_mode=pl.Buffered(1)),
                  pl.BlockSpec((n_pairs, seq, LANES), lambda b, i: (0, b, 0), pipeline_mode=pl.Buffered(1)),
                  q_tile(pool_width),
                  pl.BlockSpec((HALO, pool_width),
                               lambda b, i: (jnp.maximum((b * tiles_per_seq + i) * halo_blocks - 1, 0), 0)),
                  q_tile(gate_width),
                  _resident(pw.shape), _resident(ps.shape), _resident(wbp.shape), _resident(wba.shape),
                  _resident(wo.shape), _resident(n2.shape), _resident(wgu2.shape), _resident(wd2.shape),
                  _resident(nf.shape)],
        out_specs=q_tile(d_model),
        out_shape=jax.ShapeDtypeStruct((tokens, d_model), F32),
        scratch_shapes=[pltpu.VMEM((tm, sb_width), F32),
                        pltpu.VMEM((tm, pool_width), BF16),
                        pltpu.VMEM((tm, d_ff), BF16)],
        compiler_params=params(dimension_semantics=("arbitrary", "arbitrary")),
        name="mixer_ffn2",
    )(h1, q, k, v, xp, xp, gl, pw, ps, wbp, wba, wo, n2, wgu2, wd2, nf)
    return out


def kernel(x, ffn1_norm, ffn1_w_gate_up, ffn1_w_down, mix_norm, w_in, pool_w_group, pool_scale,
           w_branch_pool, w_branch_attn, w_out, ffn2_norm, ffn2_w_gate_up, ffn2_w_down, final_norm):
    batch, seq, d_model = x.shape
    depth = ffn1_norm.shape[0]
    row = lambda a: a.reshape(1, -1)
    h = x.reshape(batch * seq, d_model)
    for l in range(depth):
        assert l == depth - 1
        h = _layer(h, batch, seq,
                   row(ffn1_norm[l]), ffn1_w_gate_up[l].astype(BF16), ffn1_w_down[l].astype(BF16),
                   row(mix_norm[l]), w_in[l].astype(BF16),
                   pool_w_group[l].astype(BF16), row(pool_scale[l]),
                   w_branch_pool[l].astype(BF16), w_branch_attn[l].astype(BF16), w_out[l].astype(BF16),
                   row(ffn2_norm[l]), ffn2_w_gate_up[l].astype(BF16), ffn2_w_down[l].astype(BF16),
                   row(final_norm))
    return h.reshape(batch, seq, d_model)
```

```python
import functools

import jax
import jax.numpy as jnp
from jax import lax
from jax.experimental import pallas as pl
from jax.experimental.pallas import tpu as pltpu

F32 = jnp.float32
BF16 = jnp.bfloat16

RMS_EPS = 1e-6
LOG2E = 1.4426950408889634
POOL_WINDOWS = (2, 4, 8, 16)
N_SB_HEADS = 8
SB_HEAD_DIM = 64
LANES = 128
HALO = 16
TOKEN_TILE = 256
FF_CHUNK = 256
VMEM_LIMIT_BYTES = 56 * 1024 * 1024


def _rmsnorm(x, g_row):
    ms = jnp.mean(x * x, axis=-1, keepdims=True)
    return x * lax.rsqrt(ms + RMS_EPS) * g_row


def _swiglu(xn, wgu_ref, wd_ref, act_ref):
    d_ff = wd_ref.shape[0]
    for c0 in range(0, d_ff, FF_CHUNK):
        g = jnp.dot(xn, wgu_ref[:, c0:c0 + FF_CHUNK], preferred_element_type=F32)
        u = jnp.dot(xn, wgu_ref[:, d_ff + c0:d_ff + c0 + FF_CHUNK], preferred_element_type=F32)
        act_ref[:, c0:c0 + FF_CHUNK] = (g * jax.nn.sigmoid(g) * u).astype(BF16)
    return jnp.dot(act_ref[...], wd_ref[...], preferred_element_type=F32)


def _ffn_proj_kernel(x_ref, n1_ref, wgu_ref, wd_ref, nm_ref, win_ref,
                     h1_ref, xp_ref, q_ref, k_ref, v_ref, gl_ref, act_ref, *, pool_width, sb_width):
    x = x_ref[...]
    xn = _rmsnorm(x, n1_ref[...]).astype(BF16)
    h1 = x + 0.5 * _swiglu(xn, wgu_ref, wd_ref, act_ref)
    h1_ref[...] = h1
    un = _rmsnorm(h1, nm_ref[...]).astype(BF16)

    o1 = pool_width
    xp_ref[...] = jnp.dot(un, win_ref[:, 0:o1], preferred_element_type=F32)
    n_pairs = sb_width // LANES
    lane_lo = lax.broadcasted_iota(jnp.int32, (1, LANES), 1) < SB_HEAD_DIM
    for idx, (ref, scale, per_head) in enumerate(((q_ref, LOG2E / (SB_HEAD_DIM ** 0.5), True),
                                                  (k_ref, None, False), (v_ref, None, True))):
        lo = o1 + idx * sb_width
        t = jnp.dot(un, win_ref[:, lo:lo + sb_width], preferred_element_type=F32)
        if scale is not None:
            t = t * scale
        for p in range(n_pairs):
            tp = t[:, p * LANES:(p + 1) * LANES].astype(BF16)
            if per_head:
                zero = jnp.zeros_like(tp)
                ref[2 * p] = jnp.where(lane_lo, tp, zero)
                ref[2 * p + 1] = jnp.where(lane_lo, zero, tp)
            else:
                ref[p] = tp
    o4 = o1 + 3 * sb_width
    gw = gl_ref.shape[1]
    for c0 in range(0, gw, 512):
        gl_ref[:, c0:c0 + 512] = jnp.dot(un, win_ref[:, o4 + c0:o4 + c0 + 512], preferred_element_type=F32)


def _neg_abs(z):
    bits = lax.bitcast_convert_type(z, jnp.uint32) | jnp.uint32(0x80000000)
    return lax.bitcast_convert_type(bits, F32)


def _attend(q_ref, k_ref, v_ref, osb_ref, carry_ref, i):
    n_heads, tq, _ = q_ref.shape
    n_pairs = n_heads // 2
    tk = tq
    row = lax.broadcasted_iota(jnp.int32, (tq, tk), 0)
    col = lax.broadcasted_iota(jnp.int32, (tq, tk), 1)
    tri = (row > col).astype(BF16)
    diag_mask = col < row

    def block(j, diagonal):
        r0 = pl.multiple_of(j * tk, tk)
        for p in range(n_pairs):
            kb = k_ref[p, pl.ds(r0, tk), :]
            vcat = jnp.concatenate([v_ref[2 * p, pl.ds(r0, tk), :], v_ref[2 * p + 1, pl.ds(r0, tk), :]], axis=0)
            a_pair = []
            for h in range(2 * p, 2 * p + 2):
                z = lax.dot_general(q_ref[h], kb, (((1,), (1,)), ((), ())), preferred_element_type=F32)
                log_beta = jnp.minimum(z, 0.0) - jnp.log(1.0 + jnp.exp2(_neg_abs(z))) * LOG2E
                l = log_beta - z
                if diagonal:
                    l = jnp.where(diag_mask, l, 0.0)
                log_a = log_beta + jnp.dot(l.astype(BF16), tri, preferred_element_type=F32)
                row_sum = jnp.sum(l, axis=-1, keepdims=True)
                if diagonal:
                    a = jnp.where(diag_mask, jnp.exp2(log_a), 0.0)
                    carry_ref[h] = row_sum
                else:
                    a = jnp.exp2(log_a + carry_ref[h])
                    carry_ref[h] += row_sum
                a_pair.append(a.astype(BF16))
            o = jnp.dot(jnp.concatenate(a_pair, axis=1), vcat, preferred_element_type=F32)
            lanes = slice(p * LANES, (p + 1) * LANES)
            if diagonal:
                osb_ref[:, lanes] = o
            else:
                osb_ref[:, lanes] += o

    block(i, True)

    @pl.loop(0, i)
    def _(t):
        block(i - 1 - t, False)


def _mixer_ffn_kernel(h1_ref, q_ref, k_ref, v_ref, xp_ref, xph_ref, gl_ref,
                      pw_ref, ps_ref, wbp_ref, wba_ref, wo_ref, n2_ref, wgu_ref, wd_ref, nf_ref,
                      out_ref, osb_ref, carry_ref, pm_ref, act_ref):
    i = pl.program_id(1)
    tq = h1_ref.shape[0]
    d_model = h1_ref.shape[1]

    _attend(q_ref, k_ref, v_ref, osb_ref, carry_ref, i)

    halo = xph_ref[...] * (i > 0).astype(F32)
    pos = i * tq + lax.broadcasted_iota(jnp.int32, (tq, 1), 0)
    group = pw_ref.shape[1]
    for gi, w in enumerate(POOL_WINDOWS):
        lanes = slice(gi * group, (gi + 1) * group)
        xg = xp_ref[:, lanes]
        s = jnp.concatenate([halo[:, lanes], xg], axis=0)
        d = 1
        while d < w:
            s = s + pltpu.roll(s, d, 0)
            d *= 2
        count = jnp.minimum(pos + 1, w).astype(F32)
        y = s[HALO:, :] / count - xg
        yg = jnp.dot(y.astype(BF16), pw_ref[gi], preferred_element_type=F32)
        pm_ref[:, lanes] = (yg * ps_ref[:, lanes]).astype(BF16)

    y_pool = jnp.dot(pm_ref[...], wbp_ref[...], preferred_element_type=F32)
    y_sb = jnp.dot(osb_ref[...].astype(BF16), wba_ref[...], preferred_element_type=F32)
    g_pool = jax.nn.sigmoid(gl_ref[:, 0:d_model])
    g_sb = jax.nn.sigmoid(gl_ref[:, d_model:2 * d_model])
    m = (g_pool * y_pool + g_sb * y_sb).astype(BF16)
    h2 = h1_ref[...] + jnp.dot(m, wo_ref[...], preferred_element_type=F32)

    xn = _rmsnorm(h2, n2_ref[...]).astype(BF16)
    h3 = h2 + 0.5 * _swiglu(xn, wgu_ref, wd_ref, act_ref)
    out_ref[...] = _rmsnorm(h3, nf_ref[...])


def _resident(shape):
    nd = len(shape)
    return pl.BlockSpec(shape, lambda *_: (0,) * nd, pipeline_mode=pl.Buffered(1))


def _layer(x2, batch, seq, n1, wgu1, wd1, nm, win, pw, ps, wbp, wba, wo, n2, wgu2, wd2, nf):
    tokens, d_model = x2.shape
    d_ff = wd1.shape[0]
    pool_width = wbp.shape[0]
    sb_width = wba.shape[0]
    n_pairs = sb_width // LANES
    gate_width = 2 * d_model
    tm = TOKEN_TILE
    assert seq % tm == 0 and tm % HALO == 0 and d_ff % FF_CHUNK == 0 and gate_width % 512 == 0
    assert sb_width == N_SB_HEADS * SB_HEAD_DIM and max(POOL_WINDOWS) - 1 <= HALO
    n_tiles = tokens // tm
    tiles_per_seq = seq // tm
    params = functools.partial(pltpu.CompilerParams, vmem_limit_bytes=VMEM_LIMIT_BYTES)

    row_tile = lambda width: pl.BlockSpec((tm, width), lambda t: (t, 0))
    lane_tiles = lambda n: pl.BlockSpec((n, tm, LANES), lambda t: (0, t, 0))
    n_heads = 2 * n_pairs
    h1, xp, q, k, v, gl = pl.pallas_call(
        functools.partial(_ffn_proj_kernel, pool_width=pool_width, sb_width=sb_width),
        grid=(n_tiles,),
        in_specs=[row_tile(d_model), _resident(n1.shape), _resident(wgu1.shape), _resident(wd1.shape),
                  _resident(nm.shape), _resident(win.shape)],
        out_specs=[row_tile(d_model), row_tile(pool_width), lane_tiles(n_heads), lane_tiles(n_pairs),
                   lane_tiles(n_heads), row_tile(gate_width)],
        out_shape=[jax.ShapeDtypeStruct((tokens, d_model), F32),
                   jax.ShapeDtypeStruct((tokens, pool_width), F32),
                   jax.ShapeDtypeStruct((n_heads, tokens, LANES), BF16),
                   jax.ShapeDtypeStruct((n_pairs, tokens, LANES), BF16),
                   jax.ShapeDtypeStruct((n_heads, tokens, LANES), BF16),
                   jax.ShapeDtypeStruct((tokens, gate_width), F32)],
        scratch_shapes=[pltpu.VMEM((tm, d_ff), BF16)],
        compiler_params=params(dimension_semantics=("arbitrary",)),
        name="ffn1_proj",
    )(x2, n1, wgu1, wd1, nm, win)

    q_tile = lambda width: pl.BlockSpec((tm, width), lambda b, i: (b * tiles_per_seq + i, 0))
    halo_blocks = tm // HALO
    out = pl.pallas_call(
        _mixer_ffn_kernel,
        grid=(batch, tiles_per_seq),
        in_specs=[q_tile(d_model),
                  pl.BlockSpec((n_heads, tm, LANES), lambda b, i: (0, b * tiles_per_seq + i, 0)),
                  pl.BlockSpec((n_pairs, seq, LANES), lambda b, i: (0, b, 0), pipeline_mode=pl.Buffered(1)),
                  pl.BlockSpec((n_heads, seq, LANES), lambda b, i: (0, b, 0), pipeline_mode=pl.Buffered(1)),
                  q_tile(pool_width),
                  pl.BlockSpec((HALO, pool_width),
                               lambda b, i: (jnp.maximum((b * tiles_per_seq + i) * halo_blocks - 1, 0), 0)),
                  q_tile(gate_width),
                  _resident(pw.shape), _resident(ps.shape), _resident(wbp.shape), _resident(wba.shape),
                  _resident(wo.shape), _resident(n2.shape), _resident(wgu2.shape), _resident(wd2.shape),
                  _resident(nf.shape)],
        out_specs=q_tile(d_model),
        out_shape=jax.ShapeDtypeStruct((tokens, d_model), F32),
        scratch_shapes=[pltpu.VMEM((tm, sb_width), F32),
                        pltpu.VMEM((n_heads, tm, 1), F32),
                        pltpu.VMEM((tm, pool_width), BF16),
                        pltpu.VMEM((tm, d_ff), BF16)],
        compiler_params=params(dimension_semantics=("arbitrary", "arbitrary")),
        name="mixer_ffn2",
    )(h1, q, k, v, xp, xp, gl, pw, ps, wbp, wba, wo, n2, wgu2, wd2, nf)
    return out


def kernel(x, ffn1_norm, ffn1_w_gate_up, ffn1_w_down, mix_norm, w_in, pool_w_group, pool_scale,
           w_branch_pool, w_branch_attn, w_out, ffn2_norm, ffn2_w_gate_up, ffn2_w_down, final_norm):
    batch, seq, d_model = x.shape
    depth = ffn1_norm.shape[0]
    row = lambda a: a.reshape(1, -1)
    h = x.reshape(batch * seq, d_model)
    for l in range(depth):
        assert l == depth - 1
        h = _layer(h, batch, seq,
                   row(ffn1_norm[l]), ffn1_w_gate_up[l].astype(BF16), ffn1_w_down[l].astype(BF16),
                   row(mix_norm[l]), w_in[l].astype(BF16),
                   pool_w_group[l].astype(BF16), row(pool_scale[l]),
                   w_branch_pool[l].astype(BF16), w_branch_attn[l].astype(BF16), w_out[l].astype(BF16),
                   row(ffn2_norm[l]), ffn2_w_gate_up[l].astype(BF16), ffn2_w_down[l].astype(BF16),
                   row(final_norm))
    return h.reshape(batch, seq, d_model)
```

```python
import functools

import jax
import jax.numpy as jnp
from jax import lax
from jax.experimental import pallas as pl
from jax.experimental.pallas import tpu as pltpu

F32 = jnp.float32
BF16 = jnp.bfloat16

RMS_EPS = 1e-6
LOG2E = 1.4426950408889634
POOL_WINDOWS = (2, 4, 8, 16)
N_SB_HEADS = 8
SB_HEAD_DIM = 64
LANES = 128
HALO = 16
TOKEN_TILE = 256
FF_CHUNK = 256
VMEM_LIMIT_BYTES = 56 * 1024 * 1024


def _rmsnorm(x, g_row):
    ms = jnp.mean(x * x, axis=-1, keepdims=True)
    return x * lax.rsqrt(ms + RMS_EPS) * g_row


def _swiglu(xn, wgu_ref, wd_ref, act_ref):
    d_ff = wd_ref.shape[0]
    for c0 in range(0, d_ff, FF_CHUNK):
        g = jnp.dot(xn, wgu_ref[:, c0:c0 + FF_CHUNK], preferred_element_type=F32)
        u = jnp.dot(xn, wgu_ref[:, d_ff + c0:d_ff + c0 + FF_CHUNK], preferred_element_type=F32)
        act_ref[:, c0:c0 + FF_CHUNK] = (g * jax.nn.sigmoid(g) * u).astype(BF16)
    return jnp.dot(act_ref[...], wd_ref[...], preferred_element_type=F32)


def _ffn_proj_kernel(x_ref, n1_ref, wgu_ref, wd_ref, nm_ref, win_ref,
                     h1_ref, xp_ref, q_ref, kt_ref, v_ref, gl_ref, act_ref, *, pool_width, sb_width):
    x = x_ref[...]
    xn = _rmsnorm(x, n1_ref[...]).astype(BF16)
    h1 = x + 0.5 * _swiglu(xn, wgu_ref, wd_ref, act_ref)
    h1_ref[...] = h1
    un = _rmsnorm(h1, nm_ref[...]).astype(BF16)

    o1 = pool_width
    xp_ref[...] = jnp.dot(un, win_ref[:, 0:o1], preferred_element_type=F32)
    n_pairs = sb_width // LANES
    lane_lo = lax.broadcasted_iota(jnp.int32, (1, LANES), 1) < SB_HEAD_DIM

    def head_copies(ref, t):
        for p in range(n_pairs):
            tp = t[:, p * LANES:(p + 1) * LANES].astype(BF16)
            zero = jnp.zeros_like(tp)
            ref[2 * p] = jnp.where(lane_lo, tp, zero)
            ref[2 * p + 1] = jnp.where(lane_lo, zero, tp)

    proj = lambda lo, width: jnp.dot(un, win_ref[:, lo:lo + width], preferred_element_type=F32)
    head_copies(q_ref, proj(o1, sb_width) * (LOG2E / (SB_HEAD_DIM ** 0.5)))
    k = proj(o1 + sb_width, sb_width)
    for p in range(n_pairs):
        kt_ref[p, 0] = k[:, p * LANES:(p + 1) * LANES].T.astype(BF16)
    head_copies(v_ref, proj(o1 + 2 * sb_width, sb_width))
    o4 = o1 + 3 * sb_width
    gw = gl_ref.shape[1]
    for c0 in range(0, gw, 512):
        gl_ref[:, c0:c0 + 512] = proj(o4 + c0, 512)


def _attend(q_ref, kt_ref, v_ref, osb_ref, carry_ref, lb_ref, l16_ref, i):
    n_heads, tq, _ = q_ref.shape
    n_pairs = n_heads // 2
    tk = tq
    row = lax.broadcasted_iota(jnp.int32, (tq, tk), 0)
    col = lax.broadcasted_iota(jnp.int32, (tq, tk), 1)
    tri = (row > col).astype(BF16)
    diag_mask = col < row

    def scores(j, slot, diagonal):
        for h in range(n_heads):
            z = jnp.dot(q_ref[h], kt_ref[h // 2, j], preferred_element_type=F32)
            log_beta = jnp.minimum(z, 0.0) - jnp.log(1.0 + jnp.exp2(-jnp.abs(z))) * LOG2E
            l = log_beta - z
            if diagonal:
                l = jnp.where(diag_mask, l, 0.0)
                lb_ref[slot * n_heads + h] = log_beta
                carry_ref[h] = jnp.sum(l, axis=-1, keepdims=True)
            else:
                lb_ref[slot * n_heads + h] = log_beta + carry_ref[h]
                carry_ref[h] += jnp.sum(l, axis=-1, keepdims=True)
            l16_ref[slot * n_heads + h] = l.astype(BF16)

    def weights(js, diagonal):
        for p in range(n_pairs):
            a_cat, v_cat = [], []
            for slot, j in enumerate(js):
                r0 = pl.multiple_of(j * tk, tk)
                for h in range(2 * p, 2 * p + 2):
                    s = slot * n_heads + h
                    a = jnp.exp2(lb_ref[s] + jnp.dot(l16_ref[s], tri, preferred_element_type=F32))
                    if diagonal and slot == 0:
                        a = jnp.where(diag_mask, a, 0.0)
                    a_cat.append(a.astype(BF16))
                    v_cat.append(v_ref[h, pl.ds(r0, tk), :])
            o = jnp.dot(jnp.concatenate(a_cat, axis=1), jnp.concatenate(v_cat, axis=0),
                        preferred_element_type=F32)
            lanes = slice(p * LANES, (p + 1) * LANES)
            if diagonal:
                osb_ref[:, lanes] = o
            else:
                osb_ref[:, lanes] += o

    def sweep(js, diagonal):
        for slot, j in enumerate(js):
            scores(j, slot, diagonal and slot == 0)
        weights(js, diagonal)

    odd = lax.rem(i, 2)

    @pl.when(odd == 0)
    def _():
        sweep([i], True)

    @pl.when(odd == 1)
    def _():
        sweep([i, i - 1], True)

    first = i - 1 - odd

    @pl.loop(0, lax.div(i, 2))
    def _(t):
        sweep([first - 2 * t, first - 2 * t - 1], False)


def _mixer_ffn_kernel(h1_ref, q_ref, kt_ref, v_ref, xp_ref, xph_ref, gl_ref,
                      pw_ref, ps_ref, wbp_ref, wba_ref, wo_ref, n2_ref, wgu_ref, wd_ref, nf_ref,
                      out_ref, osb_ref, carry_ref, lb_ref, l16_ref, pm_ref, act_ref):
    i = pl.program_id(1)
    tq = h1_ref.shape[0]
    d_model = h1_ref.shape[1]

    _attend(q_ref, kt_ref, v_ref, osb_ref, carry_ref, lb_ref, l16_ref, i)

    halo = xph_ref[...] * (i > 0).astype(F32)
    pos = i * tq + lax.broadcasted_iota(jnp.int32, (tq, 1), 0)
    group = pw_ref.shape[1]
    for gi, w in enumerate(POOL_WINDOWS):
        lanes = slice(gi * group, (gi + 1) * group)
        xg = xp_ref[:, lanes]
        s = jnp.concatenate([halo[:, lanes], xg], axis=0)
        d = 1
        while d < w:
            s = s + pltpu.roll(s, d, 0)
            d *= 2
        count = jnp.minimum(pos + 1, w).astype(F32)
        y = s[HALO:, :] / count - xg
        yg = jnp.dot(y.astype(BF16), pw_ref[gi], preferred_element_type=F32)
        pm_ref[:, lanes] = (yg * ps_ref[:, lanes]).astype(BF16)

    y_pool = jnp.dot(pm_ref[...], wbp_ref[...], preferred_element_type=F32)
    y_sb = jnp.dot(osb_ref[...].astype(BF16), wba_ref[...], preferred_element_type=F32)
    g_pool = jax.nn.sigmoid(gl_ref[:, 0:d_model])
    g_sb = jax.nn.sigmoid(gl_ref[:, d_model:2 * d_model])
    m = (g_pool * y_pool + g_sb * y_sb).astype(BF16)
    h2 = h1_ref[...] + jnp.dot(m, wo_ref[...], preferred_element_type=F32)

    xn = _rmsnorm(h2, n2_ref[...]).astype(BF16)
    h3 = h2 + 0.5 * _swiglu(xn, wgu_ref, wd_ref, act_ref)
    out_ref[...] = _rmsnorm(h3, nf_ref[...])


def _resident(shape):
    nd = len(shape)
    return pl.BlockSpec(shape, lambda *_: (0,) * nd, pipeline_mode=pl.Buffered(1))


def _layer(x2, batch, seq, n1, wgu1, wd1, nm, win, pw, ps, wbp, wba, wo, n2, wgu2, wd2, nf):
    tokens, d_model = x2.shape
    d_ff = wd1.shape[0]
    pool_width = wbp.shape[0]
    sb_width = wba.shape[0]
    n_pairs = sb_width // LANES
    n_heads = 2 * n_pairs
    gate_width = 2 * d_model
    tm = TOKEN_TILE
    assert seq % tm == 0 and tm % HALO == 0 and d_ff % FF_CHUNK == 0 and gate_width % 512 == 0
    assert sb_width == N_SB_HEADS * SB_HEAD_DIM and max(POOL_WINDOWS) - 1 <= HALO
    n_tiles = tokens // tm
    tiles_per_seq = seq // tm
    params = functools.partial(pltpu.CompilerParams, vmem_limit_bytes=VMEM_LIMIT_BYTES)

    row_tile = lambda width: pl.BlockSpec((tm, width), lambda t: (t, 0))
    head_tiles = pl.BlockSpec((n_heads, tm, LANES), lambda t: (0, t, 0))
    h1, xp, q, kt, v, gl = pl.pallas_call(
        functools.partial(_ffn_proj_kernel, pool_width=pool_width, sb_width=sb_width),
        grid=(n_tiles,),
        in_specs=[row_tile(d_model), _resident(n1.shape), _resident(wgu1.shape), _resident(wd1.shape),
                  _resident(nm.shape), _resident(win.shape)],
        out_specs=[row_tile(d_model), row_tile(pool_width), head_tiles,
                   pl.BlockSpec((n_pairs, 1, LANES, tm), lambda t: (0, t, 0, 0)),
                   head_tiles, row_tile(gate_width)],
        out_shape=[jax.ShapeDtypeStruct((tokens, d_model), F32),
                   jax.ShapeDtypeStruct((tokens, pool_width), F32),
                   jax.ShapeDtypeStruct((n_heads, tokens, LANES), BF16),
                   jax.ShapeDtypeStruct((n_pairs, n_tiles, LANES, tm), BF16),
                   jax.ShapeDtypeStruct((n_heads, tokens, LANES), BF16),
                   jax.ShapeDtypeStruct((tokens, gate_width), F32)],
        scratch_shapes=[pltpu.VMEM((tm, d_ff), BF16)],
        compiler_params=params(dimension_semantics=("arbitrary",)),
        name="ffn1_proj",
    )(x2, n1, wgu1, wd1, nm, win)

    q_tile = lambda width: pl.BlockSpec((tm, width), lambda b, i: (b * tiles_per_seq + i, 0))
    halo_blocks = tm // HALO
    out = pl.pallas_call(
        _mixer_ffn_kernel,
        grid=(batch, tiles_per_seq),
        in_specs=[q_tile(d_model),
                  pl.BlockSpec((n_heads, tm, LANES), lambda b, i: (0, b * tiles_per_seq + i, 0)),
                  pl.BlockSpec((n_pairs, tiles_per_seq, LANES, tm), lambda b, i: (0, b, 0, 0),
                               pipeline_mode=pl.Buffered(1)),
                  pl.BlockSpec((n_heads, seq, LANES), lambda b, i: (0, b, 0), pipeline_mode=pl.Buffered(1)),
                  q_tile(pool_width),
                  pl.BlockSpec((HALO, pool_width),
                               lambda b, i: (jnp.maximum((b * tiles_per_seq + i) * halo_blocks - 1, 0), 0)),
                  q_tile(gate_width),
                  _resident(pw.shape), _resident(ps.shape), _resident(wbp.shape), _resident(wba.shape),
                  _resident(wo.shape), _resident(n2.shape), _resident(wgu2.shape), _resident(wd2.shape),
                  _resident(nf.shape)],
        out_specs=q_tile(d_model),
        out_shape=jax.ShapeDtypeStruct((tokens, d_model), F32),
        scratch_shapes=[pltpu.VMEM((tm, sb_width), F32),
                        pltpu.VMEM((n_heads, tm, 1), F32),
                        pltpu.VMEM((2 * n_heads, tm, tm), F32),
                        pltpu.VMEM((2 * n_heads, tm, tm), BF16),
                        pltpu.VMEM((tm, pool_width), BF16),
                        pltpu.VMEM((tm, d_ff), BF16)],
        compiler_params=params(dimension_semantics=("arbitrary", "arbitrary")),
        name="mixer_ffn2",
    )(h1, q, kt, v, xp, xp, gl, pw, ps, wbp, wba, wo, n2, wgu2, wd2, nf)
    return out


def kernel(x, ffn1_norm, ffn1_w_gate_up, ffn1_w_down, mix_norm, w_in, pool_w_group, pool_scale,
           w_branch_pool, w_branch_attn, w_out, ffn2_norm, ffn2_w_gate_up, ffn2_w_down, final_norm):
    batch, seq, d_model = x.shape
    depth = ffn1_norm.shape[0]
    row = lambda a: a.reshape(1, -1)
    h = x.reshape(batch * seq, d_model)
    for l in range(depth):
        assert l == depth - 1
        h = _layer(h, batch, seq,
                   row(ffn1_norm[l]), ffn1_w_gate_up[l].astype(BF16), ffn1_w_down[l].astype(BF16),
                   row(mix_norm[l]), w_in[l].astype(BF16),
                   pool_w_group[l].astype(BF16), row(pool_scale[l]),
                   w_branch_pool[l].astype(BF16), w_branch_attn[l].astype(BF16), w_out[l].astype(BF16),
                   row(ffn2_norm[l]), ffn2_w_gate_up[l].astype(BF16), ffn2_w_down[l].astype(BF16),
                   row(final_norm))
    return h.reshape(batch, seq, d_model)
```

```python
import functools

import jax
import jax.numpy as jnp
from jax import lax
from jax.experimental import pallas as pl
from jax.experimental.pallas import tpu as pltpu

F32 = jnp.float32
BF16 = jnp.bfloat16

RMS_EPS = 1e-6
LOG2E = 1.4426950408889634
POOL_WINDOWS = (2, 4, 8, 16)
N_SB_HEADS = 8
SB_HEAD_DIM = 64
LANES = 128
BF16_SUBLANES = 16
HALO = 16
TOKEN_TILE = 256
FF_CHUNK = 256
CAST_PROLOGUE_STEPS = 16
VMEM_LIMIT_BYTES = 56 * 1024 * 1024


def _rmsnorm(x, g_row):
    ms = jnp.mean(x * x, axis=-1, keepdims=True)
    return x * lax.rsqrt(ms + RMS_EPS) * g_row


def _swiglu(xn, wgu_ref, wd_ref, act_ref):
    d_ff = wd_ref.shape[0]
    for c0 in range(0, d_ff, FF_CHUNK):
        g = jnp.dot(xn, wgu_ref[:, c0:c0 + FF_CHUNK], preferred_element_type=F32)
        u = jnp.dot(xn, wgu_ref[:, d_ff + c0:d_ff + c0 + FF_CHUNK], preferred_element_type=F32)
        act_ref[:, c0:c0 + FF_CHUNK] = (g * jax.nn.sigmoid(g) * u).astype(BF16)
    return jnp.dot(act_ref[...], wd_ref[...], preferred_element_type=F32)


def _ffn_proj_kernel(x_ref, n1_ref, nm_ref, *rest, pool_width, sb_width, n_cast, n_prologue):
    own_f32, rest = rest[:3], rest[3:]
    cast_in, rest = rest[:n_cast], rest[n_cast:]
    (h1_ref, xp_ref, q_ref, kt_ref, v_ref, gl_ref), rest = rest[:6], rest[6:]
    cast_out, rest = rest[:n_cast], rest[n_cast:]
    act_ref, wgu_ref, wd_ref, win_ref = rest
    step = pl.program_id(0)

    @pl.when(step < n_prologue)
    def _():
        for src, dst in zip(own_f32, (wgu_ref, wd_ref, win_ref)):
            rows = src.shape[0]
            dst[pl.ds(pl.multiple_of(step * rows, rows), rows), :] = src[...].astype(BF16)

    @pl.when(step >= n_prologue)
    def _():
        for src, dst in zip(cast_in, cast_out):
            dst[...] = src[...].astype(BF16)
        _ffn_proj_tile(x_ref, n1_ref, wgu_ref, wd_ref, nm_ref, win_ref,
                       h1_ref, xp_ref, q_ref, kt_ref, v_ref, gl_ref, act_ref, pool_width, sb_width)


def _ffn_proj_tile(x_ref, n1_ref, wgu_ref, wd_ref, nm_ref, win_ref,
                   h1_ref, xp_ref, q_ref, kt_ref, v_ref, gl_ref, act_ref, pool_width, sb_width):
    x = x_ref[...]
    xn = _rmsnorm(x, n1_ref[...]).astype(BF16)
    h1 = x + 0.5 * _swiglu(xn, wgu_ref, wd_ref, act_ref)
    h1_ref[...] = h1
    un = _rmsnorm(h1, nm_ref[...]).astype(BF16)

    o1 = pool_width
    xp_ref[...] = jnp.dot(un, win_ref[:, 0:o1], preferred_element_type=F32)
    n_pairs = sb_width // LANES
    lane_lo = lax.broadcasted_iota(jnp.int32, (1, LANES), 1) < SB_HEAD_DIM

    def head_copies(ref, t):
        for p in range(n_pairs):
            tp = t[:, p * LANES:(p + 1) * LANES].astype(BF16)
            zero = jnp.zeros_like(tp)
            ref[2 * p] = jnp.where(lane_lo, tp, zero)
            ref[2 * p + 1] = jnp.where(lane_lo, zero, tp)

    proj = lambda lo, width: jnp.dot(un, win_ref[:, lo:lo + width], preferred_element_type=F32)
    head_copies(q_ref, proj(o1, sb_width) * (LOG2E / (SB_HEAD_DIM ** 0.5)))
    k = proj(o1 + sb_width, sb_width)
    for p in range(n_pairs):
        kt_ref[p, 0] = k[:, p * LANES:(p + 1) * LANES].T.astype(BF16)
    head_copies(v_ref, proj(o1 + 2 * sb_width, sb_width))
    o4 = o1 + 3 * sb_width
    gw = gl_ref.shape[1]
    for c0 in range(0, gw, 512):
        gl_ref[:, c0:c0 + 512] = proj(o4 + c0, 512)


def _attend(q_ref, kt_ref, v_ref, osb_ref, carry_ref, lb_ref, l16_ref, i):
    n_heads, tq, _ = q_ref.shape
    n_pairs = n_heads // 2
    tk = tq
    row = lax.broadcasted_iota(jnp.int32, (tq, tk), 0)
    col = lax.broadcasted_iota(jnp.int32, (tq, tk), 1)
    tri = (row > col).astype(BF16)
    diag_mask = col < row

    def scores(j, slot, diagonal):
        for h in range(n_heads):
            z = jnp.dot(q_ref[h], kt_ref[h // 2, j], preferred_element_type=F32)
            log_beta = jnp.minimum(z, 0.0) - jnp.log(1.0 + jnp.exp2(-jnp.abs(z))) * LOG2E
            l = log_beta - z
            if diagonal:
                l = jnp.where(diag_mask, l, 0.0)
                lb_ref[slot * n_heads + h] = log_beta
                carry_ref[h] = jnp.sum(l, axis=-1, keepdims=True)
            else:
                lb_ref[slot * n_heads + h] = log_beta + carry_ref[h]
                carry_ref[h] += jnp.sum(l, axis=-1, keepdims=True)
            l16_ref[slot * n_heads + h] = l.astype(BF16)

    def weights(js, diagonal):
        for p in range(n_pairs):
            a_cat, v_cat = [], []
            for slot, j in enumerate(js):
                r0 = pl.multiple_of(j * tk, tk)
                for h in range(2 * p, 2 * p + 2):
                    s = slot * n_heads + h
                    a = jnp.exp2(lb_ref[s] + jnp.dot(l16_ref[s], tri, preferred_element_type=F32))
                    if diagonal and slot == 0:
                        a = jnp.where(diag_mask, a, 0.0)
                    a_cat.append(a.astype(BF16))
                    v_cat.append(v_ref[h, pl.ds(r0, tk), :])
            o = jnp.dot(jnp.concatenate(a_cat, axis=1), jnp.concatenate(v_cat, axis=0),
                        preferred_element_type=F32)
            lanes = slice(p * LANES, (p + 1) * LANES)
            if diagonal:
                osb_ref[:, lanes] = o
            else:
                osb_ref[:, lanes] += o

    def sweep(js, diagonal):
        for slot, j in enumerate(js):
            scores(j, slot, diagonal and slot == 0)
        weights(js, diagonal)

    odd = lax.rem(i, 2)

    @pl.when(odd == 0)
    def _():
        sweep([i], True)

    @pl.when(odd == 1)
    def _():
        sweep([i, i - 1], True)

    first = i - 1 - odd

    @pl.loop(0, lax.div(i, 2))
    def _(t):
        sweep([first - 2 * t, first - 2 * t - 1], False)


def _mixer_ffn_kernel(h1_ref, q_ref, kt_ref, v_ref, xp_ref, xph_ref, gl_ref,
                      pw_ref, ps_ref, wbp_ref, wba_ref, wo_ref, n2_ref, wgu_ref, wd_ref, nf_ref,
                      out_ref, osb_ref, carry_ref, lb_ref, l16_ref, pm_ref, act_ref):
    i = pl.program_id(1)
    tq = h1_ref.shape[0]
    d_model = h1_ref.shape[1]

    _attend(q_ref, kt_ref, v_ref, osb_ref, carry_ref, lb_ref, l16_ref, i)

    halo = xph_ref[...] * (i > 0).astype(F32)
    pos = i * tq + lax.broadcasted_iota(jnp.int32, (tq, 1), 0)
    group = pw_ref.shape[1]
    for gi, w in enumerate(POOL_WINDOWS):
        lanes = slice(gi * group, (gi + 1) * group)
        xg = xp_ref[:, lanes]
        s = jnp.concatenate([halo[:, lanes], xg], axis=0)
        d = 1
        while d < w:
            s = s + pltpu.roll(s, d, 0)
            d *= 2
        count = jnp.minimum(pos + 1, w).astype(F32)
        y = s[HALO:, :] / count - xg
        yg = jnp.dot(y.astype(BF16), pw_ref[gi], preferred_element_type=F32)
        pm_ref[:, lanes] = (yg * ps_ref[:, lanes]).astype(BF16)

    y_pool = jnp.dot(pm_ref[...], wbp_ref[...], preferred_element_type=F32)
    y_sb = jnp.dot(osb_ref[...].astype(BF16), wba_ref[...], preferred_element_type=F32)
    g_pool = jax.nn.sigmoid(gl_ref[:, 0:d_model])
    g_sb = jax.nn.sigmoid(gl_ref[:, d_model:2 * d_model])
    m = (g_pool * y_pool + g_sb * y_sb).astype(BF16)
    h2 = h1_ref[...] + jnp.dot(m, wo_ref[...], preferred_element_type=F32)

    xn = _rmsnorm(h2, n2_ref[...]).astype(BF16)
    h3 = h2 + 0.5 * _swiglu(xn, wgu_ref, wd_ref, act_ref)
    out_ref[...] = _rmsnorm(h3, nf_ref[...])


def _resident(shape):
    nd = len(shape)
    return pl.BlockSpec(shape, lambda *_: (0,) * nd, pipeline_mode=pl.Buffered(1))


def _layer(x2, batch, seq, n1, own_weights, nm, pw, ps, late_weights, n2, nf):
    tokens, d_model = x2.shape
    d_ff = own_weights[1].shape[0]
    pool_width = late_weights[0].shape[0]
    sb_width = late_weights[1].shape[0]
    n_pairs = sb_width // LANES
    n_heads = 2 * n_pairs
    gate_width = 2 * d_model
    tm = TOKEN_TILE
    assert seq % tm == 0 and tm % HALO == 0 and d_ff % FF_CHUNK == 0 and gate_width % 512 == 0
    assert sb_width == N_SB_HEADS * SB_HEAD_DIM and max(POOL_WINDOWS) - 1 <= HALO
    n_tiles = tokens // tm
    tiles_per_seq = seq // tm
    params = functools.partial(pltpu.CompilerParams, vmem_limit_bytes=VMEM_LIMIT_BYTES)

    n_pro = CAST_PROLOGUE_STEPS
    tile = lambda s: jnp.maximum(s - n_pro, 0)
    row_tile = lambda width: pl.BlockSpec((tm, width), lambda s: (tile(s), 0))
    head_tiles = pl.BlockSpec((n_heads, tm, LANES), lambda s: (0, tile(s), 0))
    assert all(w.shape[0] % (n_pro * BF16_SUBLANES) == 0 for w in own_weights)
    own_specs = [pl.BlockSpec((w.shape[0] // n_pro, w.shape[1]), lambda s: (jnp.minimum(s, n_pro - 1), 0))
                 for w in own_weights]
    cast_rows = n_tiles * BF16_SUBLANES
    assert all(w.size % (cast_rows * LANES) == 0 for w in late_weights)
    cast_views = [w.reshape(cast_rows, w.size // cast_rows) for w in late_weights]
    cast_specs = [pl.BlockSpec((BF16_SUBLANES, w.shape[1]), lambda s: (tile(s), 0)) for w in cast_views]
    h1, xp, q, kt, v, gl, *late_bf16 = pl.pallas_call(
        functools.partial(_ffn_proj_kernel, pool_width=pool_width, sb_width=sb_width,
                          n_cast=len(late_weights), n_prologue=n_pro),
        grid=(n_pro + n_tiles,),
        in_specs=[row_tile(d_model), _resident(n1.shape), _resident(nm.shape), *own_specs, *cast_specs],
        out_specs=[row_tile(d_model), row_tile(pool_width), head_tiles,
                   pl.BlockSpec((n_pairs, 1, LANES, tm), lambda s: (0, tile(s), 0, 0)),
                   head_tiles, row_tile(gate_width), *cast_specs],
        out_shape=[jax.ShapeDtypeStruct((tokens, d_model), F32),
                   jax.ShapeDtypeStruct((tokens, pool_width), F32),
                   jax.ShapeDtypeStruct((n_heads, tokens, LANES), BF16),
                   jax.ShapeDtypeStruct((n_pairs, n_tiles, LANES, tm), BF16),
                   jax.ShapeDtypeStruct((n_heads, tokens, LANES), BF16),
                   jax.ShapeDtypeStruct((tokens, gate_width), F32),
                   *[jax.ShapeDtypeStruct(w.shape, BF16) for w in cast_views]],
        scratch_shapes=[pltpu.VMEM((tm, d_ff), BF16), *[pltpu.VMEM(w.shape, BF16) for w in own_weights]],
        compiler_params=params(dimension_semantics=("arbitrary",)),
        name="ffn1_proj",
    )(x2, n1, nm, *own_weights, *cast_views)
    wbp, wba, wo, wgu2, wd2 = [w16.reshape(w.shape) for w16, w in zip(late_bf16, late_weights)]

    q_tile = lambda width: pl.BlockSpec((tm, width), lambda b, i: (b * tiles_per_seq + i, 0))
    halo_blocks = tm // HALO
    out = pl.pallas_call(
        _mixer_ffn_kernel,
        grid=(batch, tiles_per_seq),
        in_specs=[q_tile(d_model),
                  pl.BlockSpec((n_heads, tm, LANES), lambda b, i: (0, b * tiles_per_seq + i, 0)),
                  pl.BlockSpec((n_pairs, tiles_per_seq, LANES, tm), lambda b, i: (0, b, 0, 0),
                               pipeline_mode=pl.Buffered(1)),
                  pl.BlockSpec((n_heads, seq, LANES), lambda b, i: (0, b, 0), pipeline_mode=pl.Buffered(1)),
                  q_tile(pool_width),
                  pl.BlockSpec((HALO, pool_width),
                               lambda b, i: (jnp.maximum((b * tiles_per_seq + i) * halo_blocks - 1, 0), 0)),
                  q_tile(gate_width),
                  _resident(pw.shape), _resident(ps.shape), _resident(wbp.shape), _resident(wba.shape),
                  _resident(wo.shape), _resident(n2.shape), _resident(wgu2.shape), _resident(wd2.shape),
                  _resident(nf.shape)],
        out_specs=q_tile(d_model),
        out_shape=jax.ShapeDtypeStruct((tokens, d_model), F32),
        scratch_shapes=[pltpu.VMEM((tm, sb_width), F32),
                        pltpu.VMEM((n_heads, tm, 1), F32),
                        pltpu.VMEM((2 * n_heads, tm, tm), F32),
                        pltpu.VMEM((2 * n_heads, tm, tm), BF16),
                        pltpu.VMEM((tm, pool_width), BF16),
                        pltpu.VMEM((tm, d_ff), BF16)],
        compiler_params=params(dimension_semantics=("arbitrary", "arbitrary")),
        name="mixer_ffn2",
    )(h1, q, kt, v, xp, xp, gl, pw, ps, wbp, wba, wo, n2, wgu2, wd2, nf)
    return out


def kernel(x, ffn1_norm, ffn1_w_gate_up, ffn1_w_down, mix_norm, w_in, pool_w_group, pool_scale,
           w_branch_pool, w_branch_attn, w_out, ffn2_norm, ffn2_w_gate_up, ffn2_w_down, final_norm):
    batch, seq, d_model = x.shape
    depth = ffn1_norm.shape[0]
    row = lambda a: a.reshape(1, -1)
    h = x.reshape(batch * seq, d_model)
    for l in range(depth):
        assert l == depth - 1
        h = _layer(h, batch, seq,
                   row(ffn1_norm[l]), (ffn1_w_gate_up[l], ffn1_w_down[l], w_in[l]), row(mix_norm[l]),
                   pool_w_group[l].astype(BF16), row(pool_scale[l]),
                   (w_branch_pool[l], w_branch_attn[l], w_out[l], ffn2_w_gate_up[l], ffn2_w_down[l]),
                   row(ffn2_norm[l]), row(final_norm))
    return h.reshape(batch, seq, d_model)
```

```python
import functools

import jax
import jax.numpy as jnp
from jax import lax
from jax.experimental import pallas as pl
from jax.experimental.pallas import tpu as pltpu

F32 = jnp.float32
BF16 = jnp.bfloat16

RMS_EPS = 1e-6
LOG2E = 1.4426950408889634
POOL_WINDOWS = (2, 4, 8, 16)
N_SB_HEADS = 8
SB_HEAD_DIM = 64
LANES = 128
BF16_SUBLANES = 16
HALO = 16
TOKEN_TILE = 256
FF_CHUNK = 256
CAST_SLICES = 16
VMEM_LIMIT_BYTES = 56 * 1024 * 1024


def _rmsnorm(x, g_row):
    ms = jnp.mean(x * x, axis=-1, keepdims=True)
    return x * lax.rsqrt(ms + RMS_EPS) * g_row


def _swiglu(xn, wgu_ref, wd_ref, act_ref):
    d_ff = wd_ref.shape[0]
    for c0 in range(0, d_ff, FF_CHUNK):
        g = jnp.dot(xn, wgu_ref[:, c0:c0 + FF_CHUNK], preferred_element_type=F32)
        u = jnp.dot(xn, wgu_ref[:, d_ff + c0:d_ff + c0 + FF_CHUNK], preferred_element_type=F32)
        act_ref[:, c0:c0 + FF_CHUNK] = (g * jax.nn.sigmoid(g) * u).astype(BF16)
    return jnp.dot(act_ref[...], wd_ref[...], preferred_element_type=F32)


def _ffn_proj_kernel(x_ref, n1_ref, nm_ref, *rest, pool_width, sb_width, n_cast, n_slices):
    own_f32, rest = rest[:3], rest[3:]
    cast_in, rest = rest[:n_cast], rest[n_cast:]
    (h1_ref, xp_ref, q_ref, kt_ref, v_ref, gl_ref), rest = rest[:6], rest[6:]
    cast_out, rest = rest[:n_cast], rest[n_cast:]
    act_ref, wgu_ref, wd_ref, win_ref = rest
    step = pl.program_id(0)

    @pl.when(step < n_slices)
    def _():
        for src, dst in zip(own_f32, (wgu_ref, wd_ref, win_ref)):
            rows = src.shape[0]
            dst[pl.ds(pl.multiple_of(step * rows, rows), rows), :] = src[...].astype(BF16)

    @pl.when(step >= n_slices)
    def _():
        @pl.when(step < 2 * n_slices)
        def _():
            for src, dst in zip(cast_in, cast_out):
                dst[...] = src[...].astype(BF16)

        _ffn_proj_tile(x_ref, n1_ref, wgu_ref, wd_ref, nm_ref, win_ref,
                       h1_ref, xp_ref, q_ref, kt_ref, v_ref, gl_ref, act_ref, pool_width, sb_width)


def _ffn_proj_tile(x_ref, n1_ref, wgu_ref, wd_ref, nm_ref, win_ref,
                   h1_ref, xp_ref, q_ref, kt_ref, v_ref, gl_ref, act_ref, pool_width, sb_width):
    x = x_ref[...]
    xn = _rmsnorm(x, n1_ref[...]).astype(BF16)
    h1 = x + 0.5 * _swiglu(xn, wgu_ref, wd_ref, act_ref)
    h1_ref[...] = h1
    un = _rmsnorm(h1, nm_ref[...]).astype(BF16)

    o1 = pool_width
    xp_ref[...] = jnp.dot(un, win_ref[:, 0:o1], preferred_element_type=F32)
    n_pairs = sb_width // LANES
    lane_lo = lax.broadcasted_iota(jnp.int32, (1, LANES), 1) < SB_HEAD_DIM

    def head_copies(ref, t):
        for p in range(n_pairs):
            tp = t[:, p * LANES:(p + 1) * LANES].astype(BF16)
            zero = jnp.zeros_like(tp)
            ref[2 * p] = jnp.where(lane_lo, tp, zero)
            ref[2 * p + 1] = jnp.where(lane_lo, zero, tp)

    proj = lambda lo, width: jnp.dot(un, win_ref[:, lo:lo + width], preferred_element_type=F32)
    head_copies(q_ref, proj(o1, sb_width) * (LOG2E / (SB_HEAD_DIM ** 0.5)))
    k = proj(o1 + sb_width, sb_width)
    for p in range(n_pairs):
        kt_ref[p, 0] = k[:, p * LANES:(p + 1) * LANES].T.astype(BF16)
    head_copies(v_ref, proj(o1 + 2 * sb_width, sb_width))
    o4 = o1 + 3 * sb_width
    gw = gl_ref.shape[1]
    for c0 in range(0, gw, 512):
        gl_ref[:, c0:c0 + 512] = proj(o4 + c0, 512)


def _attend(q_ref, kt_ref, v_ref, osb_ref, carry_ref, lb_ref, l16_ref, i):
    n_heads, tq, _ = q_ref.shape
    n_pairs = n_heads // 2
    tk = tq
    row = lax.broadcasted_iota(jnp.int32, (tq, tk), 0)
    col = lax.broadcasted_iota(jnp.int32, (tq, tk), 1)
    tri = (row > col).astype(BF16)
    diag_mask = col < row

    def scores(j, slot, diagonal):
        for h in range(n_heads):
            z = jnp.dot(q_ref[h], kt_ref[h // 2, j], preferred_element_type=F32)
            log_beta = jnp.minimum(z, 0.0) - jnp.log(1.0 + jnp.exp2(-jnp.abs(z))) * LOG2E
            l = log_beta - z
            if diagonal:
                l = jnp.where(diag_mask, l, 0.0)
                lb_ref[slot * n_heads + h] = log_beta
                carry_ref[h] = jnp.sum(l, axis=-1, keepdims=True)
            else:
                lb_ref[slot * n_heads + h] = log_beta + carry_ref[h]
                carry_ref[h] += jnp.sum(l, axis=-1, keepdims=True)
            l16_ref[slot * n_heads + h] = l.astype(BF16)

    def weights(js, diagonal):
        for p in range(n_pairs):
            a_cat, v_cat = [], []
            for slot, j in enumerate(js):
                r0 = pl.multiple_of(j * tk, tk)
                for h in range(2 * p, 2 * p + 2):
                    s = slot * n_heads + h
                    a = jnp.exp2(lb_ref[s] + jnp.dot(l16_ref[s], tri, preferred_element_type=F32))
                    if diagonal and slot == 0:
                        a = jnp.where(diag_mask, a, 0.0)
                    a_cat.append(a.astype(BF16))
                    v_cat.append(v_ref[h, pl.ds(r0, tk), :])
            o = jnp.dot(jnp.concatenate(a_cat, axis=1), jnp.concatenate(v_cat, axis=0),
                        preferred_element_type=F32)
            lanes = slice(p * LANES, (p + 1) * LANES)
            if diagonal:
                osb_ref[:, lanes] = o
            else:
                osb_ref[:, lanes] += o

    def sweep(js, diagonal):
        for slot, j in enumerate(js):
            scores(j, slot, diagonal and slot == 0)
        weights(js, diagonal)

    odd = lax.rem(i, 2)

    @pl.when(odd == 0)
    def _():
        sweep([i], True)

    @pl.when(odd == 1)
    def _():
        sweep([i, i - 1], True)

    first = i - 1 - odd

    @pl.loop(0, lax.div(i, 2))
    def _(t):
        sweep([first - 2 * t, first - 2 * t - 1], False)


def _mixer_ffn_kernel(h1_ref, q_ref, kt_ref, v_ref, xp_ref, xph_ref, gl_ref,
                      pw_ref, ps_ref, wbp_ref, wba_ref, wo_ref, n2_ref, wgu_ref, wd_ref, nf_ref,
                      out_ref, osb_ref, carry_ref, lb_ref, l16_ref, pm_ref, act_ref):
    i = pl.program_id(1)
    tq = h1_ref.shape[0]
    d_model = h1_ref.shape[1]

    _attend(q_ref, kt_ref, v_ref, osb_ref, carry_ref, lb_ref, l16_ref, i)

    halo = xph_ref[...] * (i > 0).astype(F32)
    pos = i * tq + lax.broadcasted_iota(jnp.int32, (tq, 1), 0)
    group = pw_ref.shape[1]
    for gi, w in enumerate(POOL_WINDOWS):
        lanes = slice(gi * group, (gi + 1) * group)
        xg = xp_ref[:, lanes]
        s = jnp.concatenate([halo[:, lanes], xg], axis=0)
        d = 1
        while d < w:
            s = s + pltpu.roll(s, d, 0)
            d *= 2
        count = jnp.minimum(pos + 1, w).astype(F32)
        y = s[HALO:, :] / count - xg
        yg = jnp.dot(y.astype(BF16), pw_ref[gi], preferred_element_type=F32)
        pm_ref[:, lanes] = (yg * ps_ref[:, lanes]).astype(BF16)

    y_pool = jnp.dot(pm_ref[...], wbp_ref[...], preferred_element_type=F32)
    y_sb = jnp.dot(osb_ref[...].astype(BF16), wba_ref[...], preferred_element_type=F32)
    g_pool = jax.nn.sigmoid(gl_ref[:, 0:d_model])
    g_sb = jax.nn.sigmoid(gl_ref[:, d_model:2 * d_model])
    m = (g_pool * y_pool + g_sb * y_sb).astype(BF16)
    h2 = h1_ref[...] + jnp.dot(m, wo_ref[...], preferred_element_type=F32)

    xn = _rmsnorm(h2, n2_ref[...]).astype(BF16)
    h3 = h2 + 0.5 * _swiglu(xn, wgu_ref, wd_ref, act_ref)
    out_ref[...] = _rmsnorm(h3, nf_ref[...])


def _resident(shape):
    nd = len(shape)
    return pl.BlockSpec(shape, lambda *_: (0,) * nd, pipeline_mode=pl.Buffered(1))


def _layer(x2, batch, seq, layer, n1, own_weights, nm, pw, ps, late_weights, n2, nf):
    tokens, d_model = x2.shape
    d_ff = own_weights[1].shape[1]
    pool_width = late_weights[0].shape[1]
    sb_width = late_weights[1].shape[1]
    n_pairs = sb_width // LANES
    n_heads = 2 * n_pairs
    gate_width = 2 * d_model
    tm = TOKEN_TILE
    assert seq % tm == 0 and tm % HALO == 0 and d_ff % FF_CHUNK == 0 and gate_width % 512 == 0
    assert sb_width == N_SB_HEADS * SB_HEAD_DIM and max(POOL_WINDOWS) - 1 <= HALO
    n_tiles = tokens // tm
    tiles_per_seq = seq // tm
    params = functools.partial(pltpu.CompilerParams, vmem_limit_bytes=VMEM_LIMIT_BYTES)

    n_sl = CAST_SLICES
    assert n_tiles >= n_sl
    assert all(w.shape[1] % (n_sl * BF16_SUBLANES) == 0 for w in (*own_weights, *late_weights))
    tile = lambda s: jnp.maximum(s - n_sl, 0)
    row_tile = lambda width: pl.BlockSpec((tm, width), lambda s: (tile(s), 0))
    head_tiles = pl.BlockSpec((n_heads, tm, LANES), lambda s: (0, tile(s), 0))
    own_slice = lambda s: jnp.minimum(s, n_sl - 1)
    late_slice = lambda s: jnp.clip(s - n_sl, 0, n_sl - 1)
    f32_slices = lambda w, which: pl.BlockSpec((None, w.shape[1] // n_sl, w.shape[2]),
                                               lambda s: (layer, which(s), 0))
    bf16_slices = [pl.BlockSpec((w.shape[1] // n_sl, w.shape[2]), lambda s: (late_slice(s), 0))
                   for w in late_weights]
    h1, xp, q, kt, v, gl, wbp, wba, wo, wgu2, wd2 = pl.pallas_call(
        functools.partial(_ffn_proj_kernel, pool_width=pool_width, sb_width=sb_width,
                          n_cast=len(late_weights), n_slices=n_sl),
        grid=(n_sl + n_tiles,),
        in_specs=[row_tile(d_model), _resident(n1.shape), _resident(nm.shape),
                  *[f32_slices(w, own_slice) for w in own_weights],
                  *[f32_slices(w, late_slice) for w in late_weights]],
        out_specs=[row_tile(d_model), row_tile(pool_width), head_tiles,
                   pl.BlockSpec((n_pairs, 1, LANES, tm), lambda s: (0, tile(s), 0, 0)),
                   head_tiles, row_tile(gate_width), *bf16_slices],
        out_shape=[jax.ShapeDtypeStruct((tokens, d_model), F32),
                   jax.ShapeDtypeStruct((tokens, pool_width), F32),
                   jax.ShapeDtypeStruct((n_heads, tokens, LANES), BF16),
                   jax.ShapeDtypeStruct((n_pairs, n_tiles, LANES, tm), BF16),
                   jax.ShapeDtypeStruct((n_heads, tokens, LANES), BF16),
                   jax.ShapeDtypeStruct((tokens, gate_width), F32),
                   *[jax.ShapeDtypeStruct(w.shape[1:], BF16) for w in late_weights]],
        scratch_shapes=[pltpu.VMEM((tm, d_ff), BF16),
                        *[pltpu.VMEM(w.shape[1:], BF16) for w in own_weights]],
        compiler_params=params(dimension_semantics=("arbitrary",)),
        name="ffn1_proj",
    )(x2, n1, nm, *own_weights, *late_weights)

    q_tile = lambda width: pl.BlockSpec((tm, width), lambda b, i: (b * tiles_per_seq + i, 0))
    halo_blocks = tm // HALO
    out = pl.pallas_call(
        _mixer_ffn_kernel,
        grid=(batch, tiles_per_seq),
        in_specs=[q_tile(d_model),
                  pl.BlockSpec((n_heads, tm, LANES), lambda b, i: (0, b * tiles_per_seq + i, 0)),
                  pl.BlockSpec((n_pairs, tiles_per_seq, LANES, tm), lambda b, i: (0, b, 0, 0),
                               pipeline_mode=pl.Buffered(1)),
                  pl.BlockSpec((n_heads, seq, LANES), lambda b, i: (0, b, 0), pipeline_mode=pl.Buffered(1)),
                  q_tile(pool_width),
                  pl.BlockSpec((HALO, pool_width),
                               lambda b, i: (jnp.maximum((b * tiles_per_seq + i) * halo_blocks - 1, 0), 0)),
                  q_tile(gate_width),
                  _resident(pw.shape), _resident(ps.shape), _resident(wbp.shape), _resident(wba.shape),
                  _resident(wo.shape), _resident(n2.shape), _resident(wgu2.shape), _resident(wd2.shape),
                  _resident(nf.shape)],
        out_specs=q_tile(d_model),
        out_shape=jax.ShapeDtypeStruct((tokens, d_model), F32),
        scratch_shapes=[pltpu.VMEM((tm, sb_width), F32),
                        pltpu.VMEM((n_heads, tm, 1), F32),
                        pltpu.VMEM((2 * n_heads, tm, tm), F32),
                        pltpu.VMEM((2 * n_heads, tm, tm), BF16),
                        pltpu.VMEM((tm, pool_width), BF16),
                        pltpu.VMEM((tm, d_ff), BF16)],
        compiler_params=params(dimension_semantics=("arbitrary", "arbitrary")),
        name="mixer_ffn2",
    )(h1, q, kt, v, xp, xp, gl, pw, ps, wbp, wba, wo, n2, wgu2, wd2, nf)
    return out


def kernel(x, ffn1_norm, ffn1_w_gate_up, ffn1_w_down, mix_norm, w_in, pool_w_group, pool_scale,
           w_branch_pool, w_branch_attn, w_out, ffn2_norm, ffn2_w_gate_up, ffn2_w_down, final_norm):
    batch, seq, d_model = x.shape
    depth = ffn1_norm.shape[0]
    row = lambda a: a.reshape(1, -1)
    h = x.reshape(batch * seq, d_model)
    for l in range(depth):
        assert l == depth - 1
        h = _layer(h, batch, seq, l,
                   row(ffn1_norm[l]), (ffn1_w_gate_up, ffn1_w_down, w_in), row(mix_norm[l]),
                   pool_w_group[l].astype(BF16), row(pool_scale[l]),
                   (w_branch_pool, w_branch_attn, w_out, ffn2_w_gate_up, ffn2_w_down),
                   row(ffn2_norm[l]), row(final_norm))
    return h.reshape(batch, seq, d_model)
```

```python
import functools

import jax
import jax.numpy as jnp
from jax import lax
from jax.experimental import pallas as pl
from jax.experimental.pallas import tpu as pltpu

F32 = jnp.float32
BF16 = jnp.bfloat16

RMS_EPS = 1e-6
LOG2E = 1.4426950408889634
POOL_WINDOWS = (2, 4, 8, 16)
N_SB_HEADS = 8
SB_HEAD_DIM = 64
LANES = 128
BF16_SUBLANES = 16
HALO = 16
TOKEN_TILE = 256
FF_CHUNK = 256
CAST_SLICES = 16
VMEM_LIMIT_BYTES = 56 * 1024 * 1024


def _rmsnorm(x, g_row):
    ms = jnp.mean(x * x, axis=-1, keepdims=True)
    return x * lax.rsqrt(ms + RMS_EPS) * g_row


def _swiglu(xn, wgu_ref, wd_ref, act_ref):
    d_ff = wd_ref.shape[0]
    for c0 in range(0, d_ff, FF_CHUNK):
        g = jnp.dot(xn, wgu_ref[:, c0:c0 + FF_CHUNK], preferred_element_type=F32)
        u = jnp.dot(xn, wgu_ref[:, d_ff + c0:d_ff + c0 + FF_CHUNK], preferred_element_type=F32)
        act_ref[:, c0:c0 + FF_CHUNK] = (g * jax.nn.sigmoid(g) * u).astype(BF16)
    return jnp.dot(act_ref[...], wd_ref[...], preferred_element_type=F32)


def _ffn_proj_kernel(x_ref, n1_ref, nm_ref, *rest, pool_width, sb_width, n_cast, n_slices):
    own_f32, rest = rest[:3], rest[3:]
    cast_in, rest = rest[:n_cast], rest[n_cast:]
    (h1_ref, xp_ref, q_ref, kt_ref, v_ref, gl_ref), rest = rest[:6], rest[6:]
    cast_out, rest = rest[:n_cast], rest[n_cast:]
    act_ref, wgu_ref, wd_ref, win_ref = rest
    step = pl.program_id(0)

    @pl.when(step < n_slices)
    def _():
        for src, dst in zip(own_f32, (wgu_ref, wd_ref, win_ref)):
            rows = src.shape[0]
            dst[pl.ds(pl.multiple_of(step * rows, rows), rows), :] = src[...].astype(BF16)

    @pl.when(step >= n_slices)
    def _():
        @pl.when(step < 2 * n_slices)
        def _():
            for src, dst in zip(cast_in, cast_out):
                dst[...] = src[...].astype(BF16)

        _ffn_proj_tile(x_ref, n1_ref, wgu_ref, wd_ref, nm_ref, win_ref,
                       h1_ref, xp_ref, q_ref, kt_ref, v_ref, gl_ref, act_ref, pool_width, sb_width)


def _ffn_proj_tile(x_ref, n1_ref, wgu_ref, wd_ref, nm_ref, win_ref,
                   h1_ref, xp_ref, q_ref, kt_ref, v_ref, gl_ref, act_ref, pool_width, sb_width):
    x = x_ref[...]
    xn = _rmsnorm(x, n1_ref[...]).astype(BF16)
    h1 = x + 0.5 * _swiglu(xn, wgu_ref, wd_ref, act_ref)
    h1_ref[...] = h1
    un = _rmsnorm(h1, nm_ref[...]).astype(BF16)

    o1 = pool_width
    xp_ref[...] = jnp.dot(un, win_ref[:, 0:o1], preferred_element_type=F32)
    n_pairs = sb_width // LANES
    lane_lo = lax.broadcasted_iota(jnp.int32, (1, LANES), 1) < SB_HEAD_DIM

    def head_copies(ref, t):
        for p in range(n_pairs):
            tp = t[:, p * LANES:(p + 1) * LANES].astype(BF16)
            zero = jnp.zeros_like(tp)
            ref[2 * p] = jnp.where(lane_lo, tp, zero)
            ref[2 * p + 1] = jnp.where(lane_lo, zero, tp)

    proj = lambda lo, width: jnp.dot(un, win_ref[:, lo:lo + width], preferred_element_type=F32)
    head_copies(q_ref, proj(o1, sb_width) * (LOG2E / (SB_HEAD_DIM ** 0.5)))
    k = proj(o1 + sb_width, sb_width)
    for p in range(n_pairs):
        kt_ref[p, 0] = k[:, p * LANES:(p + 1) * LANES].T.astype(BF16)
    head_copies(v_ref, proj(o1 + 2 * sb_width, sb_width))
    o4 = o1 + 3 * sb_width
    gw = gl_ref.shape[1]
    for c0 in range(0, gw, 512):
        gl_ref[:, c0:c0 + 512] = proj(o4 + c0, 512)


def _attend(q_ref, kt_ref, v_ref, osb_ref, carry_ref, lb_ref, l16_ref, i):
    n_heads, tq, _ = q_ref.shape
    n_pairs = n_heads // 2
    tk = tq
    row = lax.broadcasted_iota(jnp.int32, (tq, tk), 0)
    col = lax.broadcasted_iota(jnp.int32, (tq, tk), 1)
    tri = (row > col).astype(BF16)
    diag_mask = col < row

    def scores(j, slot, diagonal):
        for h in range(n_heads):
            z = jnp.dot(q_ref[h], kt_ref[h // 2, j], preferred_element_type=F32)
            log_beta = jnp.minimum(z, 0.0) - jnp.log(1.0 + jnp.exp2(-jnp.abs(z))) * LOG2E
            l = log_beta - z
            if diagonal:
                l = jnp.where(diag_mask, l, 0.0)
                lb_ref[slot * n_heads + h] = log_beta
                carry_ref[h] = jnp.sum(l, axis=-1, keepdims=True)
            else:
                lb_ref[slot * n_heads + h] = log_beta + carry_ref[h]
                carry_ref[h] += jnp.sum(l, axis=-1, keepdims=True)
            l16_ref[slot * n_heads + h] = l.astype(BF16)

    def weights(js, diagonal):
        for p in range(n_pairs):
            a_cat, v_cat = [], []
            for slot, j in enumerate(js):
                r0 = pl.multiple_of(j * tk, tk)
                for h in range(2 * p, 2 * p + 2):
                    s = slot * n_heads + h
                    a = jnp.exp2(lb_ref[s] + jnp.dot(l16_ref[s], tri, preferred_element_type=F32))
                    if diagonal and slot == 0:
                        a = jnp.where(diag_mask, a, 0.0)
                    a_cat.append(a.astype(BF16))
                    v_cat.append(v_ref[h, pl.ds(r0, tk), :])
            o = jnp.dot(jnp.concatenate(a_cat, axis=1), jnp.concatenate(v_cat, axis=0),
                        preferred_element_type=F32)
            lanes = slice(p * LANES, (p + 1) * LANES)
            if diagonal:
                osb_ref[:, lanes] = o
            else:
                osb_ref[:, lanes] += o

    def sweep(js, diagonal):
        for slot, j in enumerate(js):
            scores(j, slot, diagonal and slot == 0)
        weights(js, diagonal)

    odd = lax.rem(i, 2)

    @pl.when(odd == 0)
    def _():
        sweep([i], True)

    @pl.when(odd == 1)
    def _():
        sweep([i, i - 1], True)

    first = i - 1 - odd

    @pl.loop(0, lax.div(i, 2))
    def _(t):
        sweep([first - 2 * t, first - 2 * t - 1], False)


def _mixer_ffn_kernel(h1_ref, q_ref, kt_new_ref, v_new_ref, xp_ref, xph_ref, gl_ref,
                      pw_ref, ps_ref, wbp_ref, wba_ref, wo_ref, n2_ref, wgu_ref, wd_ref, nf_ref,
                      out_ref, kt_ref, v_ref, osb_ref, carry_ref, lb_ref, l16_ref, pm_ref, act_ref):
    i = pl.program_id(1)
    tq = h1_ref.shape[0]
    d_model = h1_ref.shape[1]

    for p in range(kt_ref.shape[0]):
        kt_ref[p, i] = kt_new_ref[p, 0]
    v_ref[:, pl.ds(pl.multiple_of(i * tq, tq), tq), :] = v_new_ref[...]

    _attend(q_ref, kt_ref, v_ref, osb_ref, carry_ref, lb_ref, l16_ref, i)

    halo = xph_ref[...] * (i > 0).astype(F32)
    pos = i * tq + lax.broadcasted_iota(jnp.int32, (tq, 1), 0)
    group = pw_ref.shape[1]
    for gi, w in enumerate(POOL_WINDOWS):
        lanes = slice(gi * group, (gi + 1) * group)
        xg = xp_ref[:, lanes]
        s = jnp.concatenate([halo[:, lanes], xg], axis=0)
        d = 1
        while d < w:
            s = s + pltpu.roll(s, d, 0)
            d *= 2
        count = jnp.minimum(pos + 1, w).astype(F32)
        y = s[HALO:, :] / count - xg
        yg = jnp.dot(y.astype(BF16), pw_ref[gi], preferred_element_type=F32)
        pm_ref[:, lanes] = (yg * ps_ref[:, lanes]).astype(BF16)

    y_pool = jnp.dot(pm_ref[...], wbp_ref[...], preferred_element_type=F32)
    y_sb = jnp.dot(osb_ref[...].astype(BF16), wba_ref[...], preferred_element_type=F32)
    g_pool = jax.nn.sigmoid(gl_ref[:, 0:d_model])
    g_sb = jax.nn.sigmoid(gl_ref[:, d_model:2 * d_model])
    m = (g_pool * y_pool + g_sb * y_sb).astype(BF16)
    h2 = h1_ref[...] + jnp.dot(m, wo_ref[...], preferred_element_type=F32)

    xn = _rmsnorm(h2, n2_ref[...]).astype(BF16)
    h3 = h2 + 0.5 * _swiglu(xn, wgu_ref, wd_ref, act_ref)
    out_ref[...] = _rmsnorm(h3, nf_ref[...])


def _resident(shape):
    nd = len(shape)
    return pl.BlockSpec(shape, lambda *_: (0,) * nd, pipeline_mode=pl.Buffered(1))


def _layer(x2, batch, seq, layer, n1, own_weights, nm, pw, ps, late_weights, n2, nf):
    tokens, d_model = x2.shape
    d_ff = own_weights[1].shape[1]
    pool_width = late_weights[0].shape[1]
    sb_width = late_weights[1].shape[1]
    n_pairs = sb_width // LANES
    n_heads = 2 * n_pairs
    gate_width = 2 * d_model
    tm = TOKEN_TILE
    assert seq % tm == 0 and tm % HALO == 0 and d_ff % FF_CHUNK == 0 and gate_width % 512 == 0
    assert sb_width == N_SB_HEADS * SB_HEAD_DIM and max(POOL_WINDOWS) - 1 <= HALO
    n_tiles = tokens // tm
    tiles_per_seq = seq // tm
    params = functools.partial(pltpu.CompilerParams, vmem_limit_bytes=VMEM_LIMIT_BYTES)

    n_sl = CAST_SLICES
    assert n_tiles >= n_sl
    assert all(w.shape[1] % (n_sl * BF16_SUBLANES) == 0 for w in (*own_weights, *late_weights))
    tile = lambda s: jnp.maximum(s - n_sl, 0)
    row_tile = lambda width: pl.BlockSpec((tm, width), lambda s: (tile(s), 0))
    head_tiles = pl.BlockSpec((n_heads, tm, LANES), lambda s: (0, tile(s), 0))
    own_slice = lambda s: jnp.minimum(s, n_sl - 1)
    late_slice = lambda s: jnp.clip(s - n_sl, 0, n_sl - 1)
    f32_slices = lambda w, which: pl.BlockSpec((None, w.shape[1] // n_sl, w.shape[2]),
                                               lambda s: (layer, which(s), 0))
    bf16_slices = [pl.BlockSpec((w.shape[1] // n_sl, w.shape[2]), lambda s: (late_slice(s), 0))
                   for w in late_weights]
    h1, xp, q, kt, v, gl, wbp, wba, wo, wgu2, wd2 = pl.pallas_call(
        functools.partial(_ffn_proj_kernel, pool_width=pool_width, sb_width=sb_width,
                          n_cast=len(late_weights), n_slices=n_sl),
        grid=(n_sl + n_tiles,),
        in_specs=[row_tile(d_model), _resident(n1.shape), _resident(nm.shape),
                  *[f32_slices(w, own_slice) for w in own_weights],
                  *[f32_slices(w, late_slice) for w in late_weights]],
        out_specs=[row_tile(d_model), row_tile(pool_width), head_tiles,
                   pl.BlockSpec((n_pairs, 1, LANES, tm), lambda s: (0, tile(s), 0, 0)),
                   head_tiles, row_tile(gate_width), *bf16_slices],
        out_shape=[jax.ShapeDtypeStruct((tokens, d_model), F32),
                   jax.ShapeDtypeStruct((tokens, pool_width), F32),
                   jax.ShapeDtypeStruct((n_heads, tokens, LANES), BF16),
                   jax.ShapeDtypeStruct((n_pairs, n_tiles, LANES, tm), BF16),
                   jax.ShapeDtypeStruct((n_heads, tokens, LANES), BF16),
                   jax.ShapeDtypeStruct((tokens, gate_width), F32),
                   *[jax.ShapeDtypeStruct(w.shape[1:], BF16) for w in late_weights]],
        scratch_shapes=[pltpu.VMEM((tm, d_ff), BF16),
                        *[pltpu.VMEM(w.shape[1:], BF16) for w in own_weights]],
        compiler_params=params(dimension_semantics=("arbitrary",)),
        name="ffn1_proj",
    )(x2, n1, nm, *own_weights, *late_weights)

    q_tile = lambda width: pl.BlockSpec((tm, width), lambda b, i: (b * tiles_per_seq + i, 0))
    halo_blocks = tm // HALO
    out = pl.pallas_call(
        _mixer_ffn_kernel,
        grid=(batch, tiles_per_seq),
        in_specs=[q_tile(d_model),
                  pl.BlockSpec((n_heads, tm, LANES), lambda b, i: (0, b * tiles_per_seq + i, 0)),
                  pl.BlockSpec((n_pairs, 1, LANES, tm), lambda b, i: (0, b * tiles_per_seq + i, 0, 0)),
                  pl.BlockSpec((n_heads, tm, LANES), lambda b, i: (0, b * tiles_per_seq + i, 0)),
                  q_tile(pool_width),
                  pl.BlockSpec((HALO, pool_width),
                               lambda b, i: (jnp.maximum((b * tiles_per_seq + i) * halo_blocks - 1, 0), 0)),
                  q_tile(gate_width),
                  _resident(pw.shape), _resident(ps.shape), _resident(wbp.shape), _resident(wba.shape),
                  _resident(wo.shape), _resident(n2.shape), _resident(wgu2.shape), _resident(wd2.shape),
                  _resident(nf.shape)],
        out_specs=q_tile(d_model),
        out_shape=jax.ShapeDtypeStruct((tokens, d_model), F32),
        scratch_shapes=[pltpu.VMEM((n_pairs, tiles_per_seq, LANES, tm), BF16),
                        pltpu.VMEM((n_heads, seq, LANES), BF16),
                        pltpu.VMEM((tm, sb_width), F32),
                        pltpu.VMEM((n_heads, tm, 1), F32),
                        pltpu.VMEM((2 * n_heads, tm, tm), F32),
                        pltpu.VMEM((2 * n_heads, tm, tm), BF16),
                        pltpu.VMEM((tm, pool_width), BF16),
                        pltpu.VMEM((tm, d_ff), BF16)],
        compiler_params=params(dimension_semantics=("arbitrary", "arbitrary")),
        name="mixer_ffn2",
    )(h1, q, kt, v, xp, xp, gl, pw, ps, wbp, wba, wo, n2, wgu2, wd2, nf)
    return out


def kernel(x, ffn1_norm, ffn1_w_gate_up, ffn1_w_down, mix_norm, w_in, pool_w_group, pool_scale,
           w_branch_pool, w_branch_attn, w_out, ffn2_norm, ffn2_w_gate_up, ffn2_w_down, final_norm):
    batch, seq, d_model = x.shape
    depth = ffn1_norm.shape[0]
    row = lambda a: a.reshape(1, -1)
    h = x.reshape(batch * seq, d_model)
    for l in range(depth):
        assert l == depth - 1
        h = _layer(h, batch, seq, l,
                   row(ffn1_norm[l]), (ffn1_w_gate_up, ffn1_w_down, w_in), row(mix_norm[l]),
                   pool_w_group[l].astype(BF16), row(pool_scale[l]),
                   (w_branch_pool, w_branch_attn, w_out, ffn2_w_gate_up, ffn2_w_down),
                   row(ffn2_norm[l]), row(final_norm))
    return h.reshape(batch, seq, d_model)
```

```python
import functools

import jax
import jax.numpy as jnp
from jax import lax
from jax.experimental import pallas as pl
from jax.experimental.pallas import tpu as pltpu

F32 = jnp.float32
BF16 = jnp.bfloat16

RMS_EPS = 1e-6
LOG2E = 1.4426950408889634
POOL_WINDOWS = (2, 4, 8, 16)
N_SB_HEADS = 8
SB_HEAD_DIM = 64
LANES = 128
BF16_SUBLANES = 16
HALO = 16
TOKEN_TILE = 256
FF_CHUNK = 256
CAST_SLICES = 16
VMEM_LIMIT_BYTES = 56 * 1024 * 1024


def _rms_scale(x):
    return lax.rsqrt(jnp.mean(x * x, axis=-1, keepdims=True) + RMS_EPS)


def _rmsnorm(x, g_row):
    return x * _rms_scale(x) * g_row


def _norm_matmul_operands(x, g_row):
    return (x * g_row).astype(BF16), _rms_scale(x)


def _swiglu(xg, r, wgu_ref, wd_ref, act_ref):
    d_ff = wd_ref.shape[0]
    for c0 in range(0, d_ff, FF_CHUNK):
        g = jnp.dot(xg, wgu_ref[:, c0:c0 + FF_CHUNK], preferred_element_type=F32) * r
        u = jnp.dot(xg, wgu_ref[:, d_ff + c0:d_ff + c0 + FF_CHUNK], preferred_element_type=F32) * r
        act_ref[:, c0:c0 + FF_CHUNK] = (g * jax.nn.sigmoid(g) * u).astype(BF16)
    return jnp.dot(act_ref[...], wd_ref[...], preferred_element_type=F32)


def _ffn_proj_kernel(x_ref, n1_ref, nm_ref, *rest, pool_width, sb_width, n_cast, n_slices):
    own_f32, rest = rest[:3], rest[3:]
    cast_in, rest = rest[:n_cast], rest[n_cast:]
    (h1_ref, xg_ref, qv_ref, kt_ref), rest = rest[:4], rest[4:]
    cast_out, rest = rest[:n_cast], rest[n_cast:]
    act_ref, wgu_ref, wd_ref, win_ref = rest
    xp_ref, gl_ref = xg_ref.at[:, 0:pool_width], xg_ref.at[:, pool_width:xg_ref.shape[1]]
    n_heads = qv_ref.shape[0] // 2
    q_ref, v_ref = qv_ref.at[0:n_heads], qv_ref.at[n_heads:2 * n_heads]
    step = pl.program_id(0)

    @pl.when(step < n_slices)
    def _():
        for src, dst in zip(own_f32, (wgu_ref, wd_ref, win_ref)):
            rows = src.shape[0]
            dst[pl.ds(pl.multiple_of(step * rows, rows), rows), :] = src[...].astype(BF16)

    @pl.when(step >= n_slices)
    def _():
        @pl.when(step < 2 * n_slices)
        def _():
            for src, dst in zip(cast_in, cast_out):
                dst[...] = src[...].astype(BF16)

        _ffn_proj_tile(x_ref, n1_ref, wgu_ref, wd_ref, nm_ref, win_ref,
                       h1_ref, xp_ref, q_ref, kt_ref, v_ref, gl_ref, act_ref, pool_width, sb_width)


def _ffn_proj_tile(x_ref, n1_ref, wgu_ref, wd_ref, nm_ref, win_ref,
                   h1_ref, xp_ref, q_ref, kt_ref, v_ref, gl_ref, act_ref, pool_width, sb_width):
    x = x_ref[...]
    h1 = x + 0.5 * _swiglu(*_norm_matmul_operands(x, n1_ref[...]), wgu_ref, wd_ref, act_ref)
    h1_ref[...] = h1
    ug, r = _norm_matmul_operands(h1, nm_ref[...])
    proj = lambda lo, width: jnp.dot(ug, win_ref[:, lo:lo + width], preferred_element_type=F32) * r

    o1 = pool_width
    xp_ref[...] = proj(0, o1)
    n_pairs = sb_width // LANES
    lane_lo = lax.broadcasted_iota(jnp.int32, (1, LANES), 1) < SB_HEAD_DIM

    def head_copies(ref, t):
        for p in range(n_pairs):
            tp = t[:, p * LANES:(p + 1) * LANES].astype(BF16)
            zero = jnp.zeros_like(tp)
            ref[2 * p] = jnp.where(lane_lo, tp, zero)
            ref[2 * p + 1] = jnp.where(lane_lo, zero, tp)

    head_copies(q_ref, proj(o1, sb_width) * (LOG2E / (SB_HEAD_DIM ** 0.5)))
    k = proj(o1 + sb_width, sb_width)
    for p in range(n_pairs):
        kt_ref[p, 0] = k[:, p * LANES:(p + 1) * LANES].T.astype(BF16)
    head_copies(v_ref, proj(o1 + 2 * sb_width, sb_width))
    o4 = o1 + 3 * sb_width
    gw = gl_ref.shape[1]
    for c0 in range(0, gw, 512):
        gl_ref[:, c0:c0 + 512] = proj(o4 + c0, 512)


def _attend(q_ref, kt_ref, v_ref, osb_ref, carry_ref, lb_ref, l16_ref, i):
    n_heads, tq, _ = q_ref.shape
    n_pairs = n_heads // 2
    tk = tq
    row = lax.broadcasted_iota(jnp.int32, (tq, tk), 0)
    col = lax.broadcasted_iota(jnp.int32, (tq, tk), 1)
    tri = (row > col).astype(BF16)
    diag_mask = col < row

    def scores(j, slot, diagonal):
        for h in range(n_heads):
            z = jnp.dot(q_ref[h], kt_ref[h // 2, j], preferred_element_type=F32)
            log_beta = jnp.minimum(z, 0.0) - jnp.log(1.0 + jnp.exp2(-jnp.abs(z))) * LOG2E
            l = log_beta - z
            if diagonal:
                l = jnp.where(diag_mask, l, 0.0)
                lb_ref[slot * n_heads + h] = log_beta
                carry_ref[h] = jnp.sum(l, axis=-1, keepdims=True)
            else:
                lb_ref[slot * n_heads + h] = log_beta + carry_ref[h]
                carry_ref[h] += jnp.sum(l, axis=-1, keepdims=True)
            l16_ref[slot * n_heads + h] = l.astype(BF16)

    def weights(js, diagonal):
        for p in range(n_pairs):
            a_cat, v_cat = [], []
            for slot, j in enumerate(js):
                r0 = pl.multiple_of(j * tk, tk)
                for h in range(2 * p, 2 * p + 2):
                    s = slot * n_heads + h
                    a = jnp.exp2(lb_ref[s] + jnp.dot(l16_ref[s], tri, preferred_element_type=F32))
                    if diagonal and slot == 0:
                        a = jnp.where(diag_mask, a, 0.0)
                    a_cat.append(a.astype(BF16))
                    v_cat.append(v_ref[h, pl.ds(r0, tk), :])
            o = jnp.dot(jnp.concatenate(a_cat, axis=1), jnp.concatenate(v_cat, axis=0),
                        preferred_element_type=F32)
            lanes = slice(p * LANES, (p + 1) * LANES)
            if diagonal:
                osb_ref[:, lanes] = o
            else:
                osb_ref[:, lanes] += o

    def sweep(js, diagonal):
        for slot, j in enumerate(js):
            scores(j, slot, diagonal and slot == 0)
        weights(js, diagonal)

    odd = lax.rem(i, 2)

    @pl.when(odd == 0)
    def _():
        sweep([i], True)

    @pl.when(odd == 1)
    def _():
        sweep([i, i - 1], True)

    first = i - 1 - odd

    @pl.loop(0, lax.div(i, 2))
    def _(t):
        sweep([first - 2 * t, first - 2 * t - 1], False)


def _mixer_ffn_kernel(h1_ref, qv_ref, kt_new_ref, xg_ref, xph_ref,
                      pw_ref, ps_ref, wbp_ref, wba_ref, wo_ref, n2_ref, wgu_ref, wd_ref, nf_ref,
                      out_ref, kt_ref, v_ref, osb_ref, carry_ref, lb_ref, l16_ref, pm_ref, act_ref):
    i = pl.program_id(1)
    tq = h1_ref.shape[0]
    d_model = h1_ref.shape[1]
    pool_width = xph_ref.shape[1]
    xp_ref, gl_ref = xg_ref.at[:, 0:pool_width], xg_ref.at[:, pool_width:xg_ref.shape[1]]
    n_heads = qv_ref.shape[0] // 2
    q_ref, v_new_ref = qv_ref.at[0:n_heads], qv_ref.at[n_heads:2 * n_heads]

    for p in range(kt_ref.shape[0]):
        kt_ref[p, i] = kt_new_ref[p, 0]
    v_ref[:, pl.ds(pl.multiple_of(i * tq, tq), tq), :] = v_new_ref[...]

    _attend(q_ref, kt_ref, v_ref, osb_ref, carry_ref, lb_ref, l16_ref, i)

    halo = xph_ref[...] * (i > 0).astype(F32)
    pos = i * tq + lax.broadcasted_iota(jnp.int32, (tq, 1), 0)
    group = pw_ref.shape[1]
    for gi, w in enumerate(POOL_WINDOWS):
        lanes = slice(gi * group, (gi + 1) * group)
        xg = xp_ref[:, lanes]
        s = jnp.concatenate([halo[:, lanes], xg], axis=0)
        d = 1
        while d < w:
            s = s + pltpu.roll(s, d, 0)
            d *= 2
        count = jnp.minimum(pos + 1, w).astype(F32)
        y = s[HALO:, :] / count - xg
        yg = jnp.dot(y.astype(BF16), pw_ref[gi], preferred_element_type=F32)
        pm_ref[:, lanes] = (yg * ps_ref[:, lanes]).astype(BF16)

    y_pool = jnp.dot(pm_ref[...], wbp_ref[...], preferred_element_type=F32)
    y_sb = jnp.dot(osb_ref[...].astype(BF16), wba_ref[...], preferred_element_type=F32)
    g_pool = jax.nn.sigmoid(gl_ref[:, 0:d_model])
    g_sb = jax.nn.sigmoid(gl_ref[:, d_model:2 * d_model])
    m = (g_pool * y_pool + g_sb * y_sb).astype(BF16)
    h2 = h1_ref[...] + jnp.dot(m, wo_ref[...], preferred_element_type=F32)

    h3 = h2 + 0.5 * _swiglu(*_norm_matmul_operands(h2, n2_ref[...]), wgu_ref, wd_ref, act_ref)
    out_ref[...] = _rmsnorm(h3, nf_ref[...])


def _resident(shape):
    nd = len(shape)
    return pl.BlockSpec(shape, lambda *_: (0,) * nd, pipeline_mode=pl.Buffered(1))


def _layer(x2, batch, seq, layer, n1, own_weights, nm, pw, ps, late_weights, n2, nf):
    tokens, d_model = x2.shape
    d_ff = own_weights[1].shape[1]
    pool_width = late_weights[0].shape[1]
    sb_width = late_weights[1].shape[1]
    n_pairs = sb_width // LANES
    n_heads = 2 * n_pairs
    gate_width = 2 * d_model
    tm = TOKEN_TILE
    assert seq % tm == 0 and tm % HALO == 0 and d_ff % FF_CHUNK == 0 and gate_width % 512 == 0
    assert sb_width == N_SB_HEADS * SB_HEAD_DIM and max(POOL_WINDOWS) - 1 <= HALO
    n_tiles = tokens // tm
    tiles_per_seq = seq // tm
    params = functools.partial(pltpu.CompilerParams, vmem_limit_bytes=VMEM_LIMIT_BYTES)

    n_sl = CAST_SLICES
    assert n_tiles >= n_sl
    assert all(w.shape[1] % (n_sl * BF16_SUBLANES) == 0 for w in (*own_weights, *late_weights))
    tile = lambda s: jnp.maximum(s - n_sl, 0)
    row_tile = lambda width: pl.BlockSpec((tm, width), lambda s: (tile(s), 0))
    head_tiles = pl.BlockSpec((2 * n_heads, tm, LANES), lambda s: (0, tile(s), 0))
    own_slice = lambda s: jnp.minimum(s, n_sl - 1)
    late_slice = lambda s: jnp.clip(s - n_sl, 0, n_sl - 1)
    f32_slices = lambda w, which: pl.BlockSpec((None, w.shape[1] // n_sl, w.shape[2]),
                                               lambda s: (layer, which(s), 0))
    bf16_slices = [pl.BlockSpec((w.shape[1] // n_sl, w.shape[2]), lambda s: (late_slice(s), 0))
                   for w in late_weights]
    h1, xg, qv, kt, wbp, wba, wo, wgu2, wd2 = pl.pallas_call(
        functools.partial(_ffn_proj_kernel, pool_width=pool_width, sb_width=sb_width,
                          n_cast=len(late_weights), n_slices=n_sl),
        grid=(n_sl + n_tiles,),
        in_specs=[row_tile(d_model), _resident(n1.shape), _resident(nm.shape),
                  *[f32_slices(w, own_slice) for w in own_weights],
                  *[f32_slices(w, late_slice) for w in late_weights]],
        out_specs=[row_tile(d_model), row_tile(pool_width + gate_width), head_tiles,
                   pl.BlockSpec((n_pairs, 1, LANES, tm), lambda s: (0, tile(s), 0, 0)), *bf16_slices],
        out_shape=[jax.ShapeDtypeStruct((tokens, d_model), F32),
                   jax.ShapeDtypeStruct((tokens, pool_width + gate_width), F32),
                   jax.ShapeDtypeStruct((2 * n_heads, tokens, LANES), BF16),
                   jax.ShapeDtypeStruct((n_pairs, n_tiles, LANES, tm), BF16),
                   *[jax.ShapeDtypeStruct(w.shape[1:], BF16) for w in late_weights]],
        scratch_shapes=[pltpu.VMEM((tm, d_ff), BF16),
                        *[pltpu.VMEM(w.shape[1:], BF16) for w in own_weights]],
        compiler_params=params(dimension_semantics=("arbitrary",)),
        name="ffn1_proj",
    )(x2, n1, nm, *own_weights, *late_weights)

    q_tile = lambda width: pl.BlockSpec((tm, width), lambda b, i: (b * tiles_per_seq + i, 0))
    halo_blocks = tm // HALO
    out = pl.pallas_call(
        _mixer_ffn_kernel,
        grid=(batch, tiles_per_seq),
        in_specs=[q_tile(d_model),
                  pl.BlockSpec((2 * n_heads, tm, LANES), lambda b, i: (0, b * tiles_per_seq + i, 0)),
                  pl.BlockSpec((n_pairs, 1, LANES, tm), lambda b, i: (0, b * tiles_per_seq + i, 0, 0)),
                  q_tile(pool_width + gate_width),
                  pl.BlockSpec((HALO, pool_width),
                               lambda b, i: (jnp.maximum((b * tiles_per_seq + i) * halo_blocks - 1, 0), 0)),
                  _resident(pw.shape), _resident(ps.shape), _resident(wbp.shape), _resident(wba.shape),
                  _resident(wo.shape), _resident(n2.shape), _resident(wgu2.shape), _resident(wd2.shape),
                  _resident(nf.shape)],
        out_specs=q_tile(d_model),
        out_shape=jax.ShapeDtypeStruct((tokens, d_model), F32),
        scratch_shapes=[pltpu.VMEM((n_pairs, tiles_per_seq, LANES, tm), BF16),
                        pltpu.VMEM((n_heads, seq, LANES), BF16),
                        pltpu.VMEM((tm, sb_width), F32),
                        pltpu.VMEM((n_heads, tm, 1), F32),
                        pltpu.VMEM((2 * n_heads, tm, tm), F32),
                        pltpu.VMEM((2 * n_heads, tm, tm), BF16),
                        pltpu.VMEM((tm, pool_width), BF16),
                        pltpu.VMEM((tm, d_ff), BF16)],
        compiler_params=params(dimension_semantics=("arbitrary", "arbitrary")),
        name="mixer_ffn2",
    )(h1, qv, kt, xg, xg, pw, ps, wbp, wba, wo, n2, wgu2, wd2, nf)
    return out


def kernel(x, ffn1_norm, ffn1_w_gate_up, ffn1_w_down, mix_norm, w_in, pool_w_group, pool_scale,
           w_branch_pool, w_branch_attn, w_out, ffn2_norm, ffn2_w_gate_up, ffn2_w_down, final_norm):
    batch, seq, d_model = x.shape
    depth = ffn1_norm.shape[0]
    row = lambda a: a.reshape(1, -1)
    h = x.reshape(batch * seq, d_model)
    for l in range(depth):
        assert l == depth - 1
        h = _layer(h, batch, seq, l,
                   row(ffn1_norm[l]), (ffn1_w_gate_up, ffn1_w_down, w_in), row(mix_norm[l]),
                   pool_w_group[l].astype(BF16), row(pool_scale[l]),
                   (w_branch_pool, w_branch_attn, w_out, ffn2_w_gate_up, ffn2_w_down),
                   row(ffn2_norm[l]), row(final_norm))
    return h.reshape(batch, seq, d_model)
```

```python
import functools

import jax
import jax.numpy as jnp
from jax import lax
from jax.experimental import pallas as pl
from jax.experimental.pallas import tpu as pltpu

F32 = jnp.float32
BF16 = jnp.bfloat16

RMS_EPS = 1e-6
LOG2E = 1.4426950408889634
POOL_WINDOWS = (2, 4, 8, 16)
N_SB_HEADS = 8
SB_HEAD_DIM = 64
LANES = 128
BF16_SUBLANES = 16
HALO = 16
TOKEN_TILE = 256
FF_CHUNK = 256
CAST_SLICES = 16
VMEM_LIMIT_BYTES = 56 * 1024 * 1024


def _rms_scale(x):
    return lax.rsqrt(jnp.mean(x * x, axis=-1, keepdims=True) + RMS_EPS)


def _rmsnorm(x, g_row):
    return x * _rms_scale(x) * g_row


def _norm_matmul_operands(x, g_row):
    return (x * g_row).astype(BF16), _rms_scale(x)


def _swiglu(xg, r, wgu_ref, wd_ref, act_ref):
    d_ff = wd_ref.shape[0]
    for c0 in range(0, d_ff, FF_CHUNK):
        g = jnp.dot(xg, wgu_ref[:, c0:c0 + FF_CHUNK], preferred_element_type=F32) * r
        u = jnp.dot(xg, wgu_ref[:, d_ff + c0:d_ff + c0 + FF_CHUNK], preferred_element_type=F32) * r
        act_ref[:, c0:c0 + FF_CHUNK] = (g * jax.nn.sigmoid(g) * u).astype(BF16)
    return jnp.dot(act_ref[...], wd_ref[...], preferred_element_type=F32)


def _ffn_proj_kernel(x_ref, n1_ref, nm_ref, *rest, pool_width, sb_width, n_cast, n_slices):
    own_f32, rest = rest[:3], rest[3:]
    cast_in, rest = rest[:n_cast], rest[n_cast:]
    (h1_ref, xg_ref, qv_ref, kt_ref), rest = rest[:4], rest[4:]
    cast_out, rest = rest[:n_cast], rest[n_cast:]
    act_ref, wgu_ref, wd_ref, win_ref = rest
    xp_ref, gl_ref = xg_ref.at[:, 0:pool_width], xg_ref.at[:, pool_width:xg_ref.shape[1]]
    n_heads = qv_ref.shape[0] // 2
    q_ref, v_ref = qv_ref.at[0:n_heads], qv_ref.at[n_heads:2 * n_heads]
    step = pl.program_id(0)

    @pl.when(step < n_slices)
    def _():
        for src, dst in zip(own_f32, (wgu_ref, wd_ref, win_ref)):
            rows = src.shape[0]
            dst[pl.ds(pl.multiple_of(step * rows, rows), rows), :] = src[...].astype(BF16)

    @pl.when(step >= n_slices)
    def _():
        @pl.when(step < 2 * n_slices)
        def _():
            for src, dst in zip(cast_in, cast_out):
                dst[...] = src[...].astype(BF16)

        _ffn_proj_tile(x_ref, n1_ref, wgu_ref, wd_ref, nm_ref, win_ref,
                       h1_ref, xp_ref, q_ref, kt_ref, v_ref, gl_ref, act_ref, pool_width, sb_width)


def _ffn_proj_tile(x_ref, n1_ref, wgu_ref, wd_ref, nm_ref, win_ref,
                   h1_ref, xp_ref, q_ref, kt_ref, v_ref, gl_ref, act_ref, pool_width, sb_width):
    x = x_ref[...]
    h1 = x + 0.5 * _swiglu(*_norm_matmul_operands(x, n1_ref[...]), wgu_ref, wd_ref, act_ref)
    h1_ref[...] = h1
    ug, r = _norm_matmul_operands(h1, nm_ref[...])
    proj = lambda lo, width: jnp.dot(ug, win_ref[:, lo:lo + width], preferred_element_type=F32) * r

    o1 = pool_width
    xp_ref[...] = proj(0, o1)
    n_pairs = sb_width // LANES
    lane_lo = lax.broadcasted_iota(jnp.int32, (1, LANES), 1) < SB_HEAD_DIM

    def head_copies(ref, t):
        for p in range(n_pairs):
            tp = t[:, p * LANES:(p + 1) * LANES].astype(BF16)
            zero = jnp.zeros_like(tp)
            ref[2 * p] = jnp.where(lane_lo, tp, zero)
            ref[2 * p + 1] = jnp.where(lane_lo, zero, tp)

    head_copies(q_ref, proj(o1, sb_width) * (LOG2E / (SB_HEAD_DIM ** 0.5)))
    k = proj(o1 + sb_width, sb_width)
    for p in range(n_pairs):
        kt_ref[p, 0] = k[:, p * LANES:(p + 1) * LANES].T.astype(BF16)
    head_copies(v_ref, proj(o1 + 2 * sb_width, sb_width))
    o4 = o1 + 3 * sb_width
    gw = gl_ref.shape[1]
    for c0 in range(0, gw, 512):
        gl_ref[:, c0:c0 + 512] = proj(o4 + c0, 512)


def _attend(q_ref, kt_ref, v_ref, osb_ref, carry_ref, lb_ref, l16_ref, i):
    n_heads, tq, _ = q_ref.shape
    n_pairs = n_heads // 2
    tk = tq
    row = lax.broadcasted_iota(jnp.int32, (tq, tk), 0)
    col = lax.broadcasted_iota(jnp.int32, (tq, tk), 1)
    tri = (row > col).astype(BF16)
    diag_mask = col < row

    def log_terms(h, j, masked):
        z = jnp.dot(q_ref[h], kt_ref[h // 2, j], preferred_element_type=F32)
        log_beta = jnp.minimum(z, 0.0) - jnp.log(1.0 + jnp.exp2(-jnp.abs(z))) * LOG2E
        l = log_beta - z
        return log_beta, (jnp.where(diag_mask, l, 0.0) if masked else l)

    def reverse_cumsum(l16):
        return jnp.dot(l16, tri, preferred_element_type=F32)

    def weigh_values(p, js, a_cat, first):
        v_cat = [v_ref[h, pl.ds(pl.multiple_of(j * tk, tk), tk), :] for j in js for h in (2 * p, 2 * p + 1)]
        o = jnp.dot(jnp.concatenate(a_cat, axis=1), jnp.concatenate(v_cat, axis=0),
                    preferred_element_type=F32)
        lanes = slice(p * LANES, (p + 1) * LANES)
        if first:
            osb_ref[:, lanes] = o
        else:
            osb_ref[:, lanes] += o

    def sweep_from_diagonal(js):
        for p in range(n_pairs):
            a_cat = []
            for slot, j in enumerate(js):
                for h in (2 * p, 2 * p + 1):
                    log_beta, l = log_terms(h, j, masked=slot == 0)
                    log_a = log_beta + reverse_cumsum(l.astype(BF16))
                    if slot == 0:
                        a = jnp.where(diag_mask, jnp.exp2(log_a), 0.0)
                        carry_ref[h] = jnp.sum(l, axis=-1, keepdims=True)
                    else:
                        a = jnp.exp2(log_a + carry_ref[h])
                        carry_ref[h] += jnp.sum(l, axis=-1, keepdims=True)
                    a_cat.append(a.astype(BF16))
            weigh_values(p, js, a_cat, first=True)

    def sweep_left(js):
        for slot, j in enumerate(js):
            for h in range(n_heads):
                log_beta, l = log_terms(h, j, masked=False)
                lb_ref[slot * n_heads + h] = log_beta + carry_ref[h]
                carry_ref[h] += jnp.sum(l, axis=-1, keepdims=True)
                l16_ref[slot * n_heads + h] = l.astype(BF16)
        for p in range(n_pairs):
            a_cat = [jnp.exp2(lb_ref[s] + reverse_cumsum(l16_ref[s])).astype(BF16)
                     for slot in range(len(js)) for s in (slot * n_heads + 2 * p, slot * n_heads + 2 * p + 1)]
            weigh_values(p, js, a_cat, first=False)

    odd = lax.rem(i, 2)

    @pl.when(odd == 0)
    def _():
        sweep_from_diagonal([i])

    @pl.when(odd == 1)
    def _():
        sweep_from_diagonal([i, i - 1])

    first = i - 1 - odd

    @pl.loop(0, lax.div(i, 2))
    def _(t):
        sweep_left([first - 2 * t, first - 2 * t - 1])


def _mixer_ffn_kernel(h1_ref, qv_ref, kt_new_ref, xg_ref, xph_ref,
                      pw_ref, ps_ref, wbp_ref, wba_ref, wo_ref, n2_ref, wgu_ref, wd_ref, nf_ref,
                      out_ref, kt_ref, v_ref, osb_ref, carry_ref, lb_ref, l16_ref, pm_ref, act_ref):
    i = pl.program_id(1)
    tq = h1_ref.shape[0]
    d_model = h1_ref.shape[1]
    pool_width = xph_ref.shape[1]
    xp_ref, gl_ref = xg_ref.at[:, 0:pool_width], xg_ref.at[:, pool_width:xg_ref.shape[1]]
    n_heads = qv_ref.shape[0] // 2
    q_ref, v_new_ref = qv_ref.at[0:n_heads], qv_ref.at[n_heads:2 * n_heads]

    for p in range(kt_ref.shape[0]):
        kt_ref[p, i] = kt_new_ref[p, 0]
    v_ref[:, pl.ds(pl.multiple_of(i * tq, tq), tq), :] = v_new_ref[...]

    _attend(q_ref, kt_ref, v_ref, osb_ref, carry_ref, lb_ref, l16_ref, i)

    halo = xph_ref[...] * (i > 0).astype(F32)
    pos = i * tq + lax.broadcasted_iota(jnp.int32, (tq, 1), 0)
    group = pw_ref.shape[1]
    for gi, w in enumerate(POOL_WINDOWS):
        lanes = slice(gi * group, (gi + 1) * group)
        xg = xp_ref[:, lanes]
        s = jnp.concatenate([halo[:, lanes], xg], axis=0)
        d = 1
        while d < w:
            s = s + pltpu.roll(s, d, 0)
            d *= 2
        count = jnp.minimum(pos + 1, w).astype(F32)
        y = s[HALO:, :] / count - xg
        yg = jnp.dot(y.astype(BF16), pw_ref[gi], preferred_element_type=F32)
        pm_ref[:, lanes] = (yg * ps_ref[:, lanes]).astype(BF16)

    y_pool = jnp.dot(pm_ref[...], wbp_ref[...], preferred_element_type=F32)
    y_sb = jnp.dot(osb_ref[...].astype(BF16), wba_ref[...], preferred_element_type=F32)
    g_pool = jax.nn.sigmoid(gl_ref[:, 0:d_model])
    g_sb = jax.nn.sigmoid(gl_ref[:, d_model:2 * d_model])
    m = (g_pool * y_pool + g_sb * y_sb).astype(BF16)
    h2 = h1_ref[...] + jnp.dot(m, wo_ref[...], preferred_element_type=F32)

    h3 = h2 + 0.5 * _swiglu(*_norm_matmul_operands(h2, n2_ref[...]), wgu_ref, wd_ref, act_ref)
    out_ref[...] = _rmsnorm(h3, nf_ref[...])


def _resident(shape):
    nd = len(shape)
    return pl.BlockSpec(shape, lambda *_: (0,) * nd, pipeline_mode=pl.Buffered(1))


def _layer(x2, batch, seq, layer, n1, own_weights, nm, pw, ps, late_weights, n2, nf):
    tokens, d_model = x2.shape
    d_ff = own_weights[1].shape[1]
    pool_width = late_weights[0].shape[1]
    sb_width = late_weights[1].shape[1]
    n_pairs = sb_width // LANES
    n_heads = 2 * n_pairs
    gate_width = 2 * d_model
    tm = TOKEN_TILE
    assert seq % tm == 0 and tm % HALO == 0 and d_ff % FF_CHUNK == 0 and gate_width % 512 == 0
    assert sb_width == N_SB_HEADS * SB_HEAD_DIM and max(POOL_WINDOWS) - 1 <= HALO
    n_tiles = tokens // tm
    tiles_per_seq = seq // tm
    params = functools.partial(pltpu.CompilerParams, vmem_limit_bytes=VMEM_LIMIT_BYTES)

    n_sl = CAST_SLICES
    assert n_tiles >= n_sl
    assert all(w.shape[1] % (n_sl * BF16_SUBLANES) == 0 for w in (*own_weights, *late_weights))
    tile = lambda s: jnp.maximum(s - n_sl, 0)
    row_tile = lambda width: pl.BlockSpec((tm, width), lambda s: (tile(s), 0))
    head_tiles = pl.BlockSpec((2 * n_heads, tm, LANES), lambda s: (0, tile(s), 0))
    own_slice = lambda s: jnp.minimum(s, n_sl - 1)
    late_slice = lambda s: jnp.clip(s - n_sl, 0, n_sl - 1)
    f32_slices = lambda w, which: pl.BlockSpec((None, w.shape[1] // n_sl, w.shape[2]),
                                               lambda s: (layer, which(s), 0))
    bf16_slices = [pl.BlockSpec((w.shape[1] // n_sl, w.shape[2]), lambda s: (late_slice(s), 0))
                   for w in late_weights]
    h1, xg, qv, kt, wbp, wba, wo, wgu2, wd2 = pl.pallas_call(
        functools.partial(_ffn_proj_kernel, pool_width=pool_width, sb_width=sb_width,
                          n_cast=len(late_weights), n_slices=n_sl),
        grid=(n_sl + n_tiles,),
        in_specs=[row_tile(d_model), _resident(n1.shape), _resident(nm.shape),
                  *[f32_slices(w, own_slice) for w in own_weights],
                  *[f32_slices(w, late_slice) for w in late_weights]],
        out_specs=[row_tile(d_model), row_tile(pool_width + gate_width), head_tiles,
                   pl.BlockSpec((n_pairs, 1, LANES, tm), lambda s: (0, tile(s), 0, 0)), *bf16_slices],
        out_shape=[jax.ShapeDtypeStruct((tokens, d_model), F32),
                   jax.ShapeDtypeStruct((tokens, pool_width + gate_width), F32),
                   jax.ShapeDtypeStruct((2 * n_heads, tokens, LANES), BF16),
                   jax.ShapeDtypeStruct((n_pairs, n_tiles, LANES, tm), BF16),
                   *[jax.ShapeDtypeStruct(w.shape[1:], BF16) for w in late_weights]],
        scratch_shapes=[pltpu.VMEM((tm, d_ff), BF16),
                        *[pltpu.VMEM(w.shape[1:], BF16) for w in own_weights]],
        compiler_params=params(dimension_semantics=("arbitrary",)),
        name="ffn1_proj",
    )(x2, n1, nm, *own_weights, *late_weights)

    q_tile = lambda width: pl.BlockSpec((tm, width), lambda b, i: (b * tiles_per_seq + i, 0))
    halo_blocks = tm // HALO
    out = pl.pallas_call(
        _mixer_ffn_kernel,
        grid=(batch, tiles_per_seq),
        in_specs=[q_tile(d_model),
                  pl.BlockSpec((2 * n_heads, tm, LANES), lambda b, i: (0, b * tiles_per_seq + i, 0)),
                  pl.BlockSpec((n_pairs, 1, LANES, tm), lambda b, i: (0, b * tiles_per_seq + i, 0, 0)),
                  q_tile(pool_width + gate_width),
                  pl.BlockSpec((HALO, pool_width),
                               lambda b, i: (jnp.maximum((b * tiles_per_seq + i) * halo_blocks - 1, 0), 0)),
                  _resident(pw.shape), _resident(ps.shape), _resident(wbp.shape), _resident(wba.shape),
                  _resident(wo.shape), _resident(n2.shape), _resident(wgu2.shape), _resident(wd2.shape),
                  _resident(nf.shape)],
        out_specs=q_tile(d_model),
        out_shape=jax.ShapeDtypeStruct((tokens, d_model), F32),
        scratch_shapes=[pltpu.VMEM((n_pairs, tiles_per_seq, LANES, tm), BF16),
                        pltpu.VMEM((n_heads, seq, LANES), BF16),
                        pltpu.VMEM((tm, sb_width), F32),
                        pltpu.VMEM((n_heads, tm, 1), F32),
                        pltpu.VMEM((2 * n_heads, tm, tm), F32),
                        pltpu.VMEM((2 * n_heads, tm, tm), BF16),
                        pltpu.VMEM((tm, pool_width), BF16),
                        pltpu.VMEM((tm, d_ff), BF16)],
        compiler_params=params(dimension_semantics=("arbitrary", "arbitrary")),
        name="mixer_ffn2",
    )(h1, qv, kt, xg, xg, pw, ps, wbp, wba, wo, n2, wgu2, wd2, nf)
    return out


def kernel(x, ffn1_norm, ffn1_w_gate_up, ffn1_w_down, mix_norm, w_in, pool_w_group, pool_scale,
           w_branch_pool, w_branch_attn, w_out, ffn2_norm, ffn2_w_gate_up, ffn2_w_down, final_norm):
    batch, seq, d_model = x.shape
    depth = ffn1_norm.shape[0]
    row = lambda a: a.reshape(1, -1)
    h = x.reshape(batch * seq, d_model)
    for l in range(depth):
        assert l == depth - 1
        h = _layer(h, batch, seq, l,
                   row(ffn1_norm[l]), (ffn1_w_gate_up, ffn1_w_down, w_in), row(mix_norm[l]),
                   pool_w_group[l].astype(BF16), row(pool_scale[l]),
                   (w_branch_pool, w_branch_attn, w_out, ffn2_w_gate_up, ffn2_w_down),
                   row(ffn2_norm[l]), row(final_norm))
    return h.reshape(batch, seq, d_model)
```

```python
import functools

import jax
import jax.numpy as jnp
from jax import lax
from jax.experimental import pallas as pl
from jax.experimental.pallas import tpu as pltpu

F32 = jnp.float32
BF16 = jnp.bfloat16

RMS_EPS = 1e-6
LOG2E = 1.4426950408889634
POOL_WINDOWS = (2, 4, 8, 16)
N_SB_HEADS = 8
SB_HEAD_DIM = 64
LANES = 128
BF16_SUBLANES = 16
HALO = 16
TOKEN_TILE = 256
FF_CHUNK = 256
DIAGONAL_SWEEP_BLOCKS = 3
CAST_SLICES = 16
VMEM_LIMIT_BYTES = 56 * 1024 * 1024


def _rms_scale(x):
    return lax.rsqrt(jnp.mean(x * x, axis=-1, keepdims=True) + RMS_EPS)


def _rmsnorm(x, g_row):
    return x * _rms_scale(x) * g_row


def _norm_matmul_operands(x, g_row):
    return (x * g_row).astype(BF16), _rms_scale(x)


def _swiglu(xg, r, wgu_ref, wd_ref, act_ref):
    d_ff = wd_ref.shape[0]
    for c0 in range(0, d_ff, FF_CHUNK):
        g = jnp.dot(xg, wgu_ref[:, c0:c0 + FF_CHUNK], preferred_element_type=F32) * r
        u = jnp.dot(xg, wgu_ref[:, d_ff + c0:d_ff + c0 + FF_CHUNK], preferred_element_type=F32) * r
        act_ref[:, c0:c0 + FF_CHUNK] = (g * jax.nn.sigmoid(g) * u).astype(BF16)
    return jnp.dot(act_ref[...], wd_ref[...], preferred_element_type=F32)


def _ffn_proj_kernel(x_ref, n1_ref, nm_ref, *rest, pool_width, sb_width, n_cast, n_slices):
    own_f32, rest = rest[:3], rest[3:]
    cast_in, rest = rest[:n_cast], rest[n_cast:]
    (h1_ref, xg_ref, qv_ref, kt_ref), rest = rest[:4], rest[4:]
    cast_out, rest = rest[:n_cast], rest[n_cast:]
    act_ref, wgu_ref, wd_ref, win_ref = rest
    xp_ref, gl_ref = xg_ref.at[:, 0:pool_width], xg_ref.at[:, pool_width:xg_ref.shape[1]]
    n_heads = qv_ref.shape[0] // 2
    q_ref, v_ref = qv_ref.at[0:n_heads], qv_ref.at[n_heads:2 * n_heads]
    step = pl.program_id(0)

    @pl.when(step < n_slices)
    def _():
        for src, dst in zip(own_f32, (wgu_ref, wd_ref, win_ref)):
            rows = src.shape[0]
            dst[pl.ds(pl.multiple_of(step * rows, rows), rows), :] = src[...].astype(BF16)

    @pl.when(step >= n_slices)
    def _():
        @pl.when(step < 2 * n_slices)
        def _():
            for src, dst in zip(cast_in, cast_out):
                dst[...] = src[...].astype(BF16)

        _ffn_proj_tile(x_ref, n1_ref, wgu_ref, wd_ref, nm_ref, win_ref,
                       h1_ref, xp_ref, q_ref, kt_ref, v_ref, gl_ref, act_ref, pool_width, sb_width)


def _ffn_proj_tile(x_ref, n1_ref, wgu_ref, wd_ref, nm_ref, win_ref,
                   h1_ref, xp_ref, q_ref, kt_ref, v_ref, gl_ref, act_ref, pool_width, sb_width):
    x = x_ref[...]
    h1 = x + 0.5 * _swiglu(*_norm_matmul_operands(x, n1_ref[...]), wgu_ref, wd_ref, act_ref)
    h1_ref[...] = h1
    ug, r = _norm_matmul_operands(h1, nm_ref[...])
    proj = lambda lo, width: jnp.dot(ug, win_ref[:, lo:lo + width], preferred_element_type=F32) * r

    o1 = pool_width
    xp_ref[...] = proj(0, o1)
    n_pairs = sb_width // LANES
    lane_lo = lax.broadcasted_iota(jnp.int32, (1, LANES), 1) < SB_HEAD_DIM

    def head_copies(ref, t):
        for p in range(n_pairs):
            tp = t[:, p * LANES:(p + 1) * LANES].astype(BF16)
            zero = jnp.zeros_like(tp)
            ref[2 * p] = jnp.where(lane_lo, tp, zero)
            ref[2 * p + 1] = jnp.where(lane_lo, zero, tp)

    head_copies(q_ref, proj(o1, sb_width) * (LOG2E / (SB_HEAD_DIM ** 0.5)))
    k = proj(o1 + sb_width, sb_width)
    for p in range(n_pairs):
        kt_ref[p, 0] = k[:, p * LANES:(p + 1) * LANES].T.astype(BF16)
    head_copies(v_ref, proj(o1 + 2 * sb_width, sb_width))
    o4 = o1 + 3 * sb_width
    gw = gl_ref.shape[1]
    for c0 in range(0, gw, 512):
        gl_ref[:, c0:c0 + 512] = proj(o4 + c0, 512)


def _attend(q_ref, kt_ref, v_ref, osb_ref, carry_ref, lb_ref, l16_ref, i):
    n_heads, tq, _ = q_ref.shape
    n_pairs = n_heads // 2
    tk = tq
    row = lax.broadcasted_iota(jnp.int32, (tq, tk), 0)
    col = lax.broadcasted_iota(jnp.int32, (tq, tk), 1)
    tri = (row > col).astype(BF16)
    diag_mask = col < row

    def log_terms(h, j, masked):
        z = jnp.dot(q_ref[h], kt_ref[h // 2, j], preferred_element_type=F32)
        log_beta = jnp.minimum(z, 0.0) - jnp.log(1.0 + jnp.exp2(-jnp.abs(z))) * LOG2E
        l = log_beta - z
        return log_beta, (jnp.where(diag_mask, l, 0.0) if masked else l)

    def reverse_cumsum(l16):
        return jnp.dot(l16, tri, preferred_element_type=F32)

    def weigh_values(p, js, a_cat, first):
        v_cat = [v_ref[h, pl.ds(pl.multiple_of(j * tk, tk), tk), :] for j in js for h in (2 * p, 2 * p + 1)]
        o = jnp.dot(jnp.concatenate(a_cat, axis=1), jnp.concatenate(v_cat, axis=0),
                    preferred_element_type=F32)
        lanes = slice(p * LANES, (p + 1) * LANES)
        if first:
            osb_ref[:, lanes] = o
        else:
            osb_ref[:, lanes] += o

    def sweep_from_diagonal(js):
        for p in range(n_pairs):
            a_cat = []
            for slot, j in enumerate(js):
                for h in (2 * p, 2 * p + 1):
                    log_beta, l = log_terms(h, j, masked=slot == 0)
                    log_a = log_beta + reverse_cumsum(l.astype(BF16))
                    if slot == 0:
                        a = jnp.where(diag_mask, jnp.exp2(log_a), 0.0)
                        carry_ref[h] = jnp.sum(l, axis=-1, keepdims=True)
                    else:
                        a = jnp.exp2(log_a + carry_ref[h])
                        carry_ref[h] += jnp.sum(l, axis=-1, keepdims=True)
                    a_cat.append(a.astype(BF16))
            weigh_values(p, js, a_cat, first=True)

    def sweep_left(js):
        for slot, j in enumerate(js):
            for h in range(n_heads):
                log_beta, l = log_terms(h, j, masked=False)
                lb_ref[slot * n_heads + h] = log_beta + carry_ref[h]
                carry_ref[h] += jnp.sum(l, axis=-1, keepdims=True)
                l16_ref[slot * n_heads + h] = l.astype(BF16)
        for p in range(n_pairs):
            a_cat = [jnp.exp2(lb_ref[s] + reverse_cumsum(l16_ref[s])).astype(BF16)
                     for slot in range(len(js)) for s in (slot * n_heads + 2 * p, slot * n_heads + 2 * p + 1)]
            weigh_values(p, js, a_cat, first=False)

    n_first = jnp.minimum(i + 1, DIAGONAL_SWEEP_BLOCKS - 1 + lax.rem(i + DIAGONAL_SWEEP_BLOCKS, 2))
    for n in range(1, DIAGONAL_SWEEP_BLOCKS + 1):
        @pl.when(n_first == n)
        def _(n=n):
            sweep_from_diagonal([i - d for d in range(n)])

    first = i - n_first

    @pl.loop(0, lax.div(first + 1, 2))
    def _(t):
        sweep_left([first - 2 * t, first - 2 * t - 1])


def _mixer_ffn_kernel(h1_ref, qv_ref, kt_new_ref, xg_ref, xph_ref,
                      pw_ref, ps_ref, wbp_ref, wba_ref, wo_ref, n2_ref, wgu_ref, wd_ref, nf_ref,
                      out_ref, kt_ref, v_ref, osb_ref, carry_ref, lb_ref, l16_ref, pm_ref, act_ref):
    i = pl.program_id(1)
    tq = h1_ref.shape[0]
    d_model = h1_ref.shape[1]
    pool_width = xph_ref.shape[1]
    xp_ref, gl_ref = xg_ref.at[:, 0:pool_width], xg_ref.at[:, pool_width:xg_ref.shape[1]]
    n_heads = qv_ref.shape[0] // 2
    q_ref, v_new_ref = qv_ref.at[0:n_heads], qv_ref.at[n_heads:2 * n_heads]

    for p in range(kt_ref.shape[0]):
        kt_ref[p, i] = kt_new_ref[p, 0]
    v_ref[:, pl.ds(pl.multiple_of(i * tq, tq), tq), :] = v_new_ref[...]

    _attend(q_ref, kt_ref, v_ref, osb_ref, carry_ref, lb_ref, l16_ref, i)

    halo = xph_ref[...] * (i > 0).astype(F32)
    pos = i * tq + lax.broadcasted_iota(jnp.int32, (tq, 1), 0)
    group = pw_ref.shape[1]
    for gi, w in enumerate(POOL_WINDOWS):
        lanes = slice(gi * group, (gi + 1) * group)
        xg = xp_ref[:, lanes]
        s = jnp.concatenate([halo[:, lanes], xg], axis=0)
        d = 1
        while d < w:
            s = s + pltpu.roll(s, d, 0)
            d *= 2
        count = jnp.minimum(pos + 1, w).astype(F32)
        y = s[HALO:, :] / count - xg
        yg = jnp.dot(y.astype(BF16), pw_ref[gi], preferred_element_type=F32)
        pm_ref[:, lanes] = (yg * ps_ref[:, lanes]).astype(BF16)

    y_pool = jnp.dot(pm_ref[...], wbp_ref[...], preferred_element_type=F32)
    y_sb = jnp.dot(osb_ref[...].astype(BF16), wba_ref[...], preferred_element_type=F32)
    g_pool = jax.nn.sigmoid(gl_ref[:, 0:d_model])
    g_sb = jax.nn.sigmoid(gl_ref[:, d_model:2 * d_model])
    m = (g_pool * y_pool + g_sb * y_sb).astype(BF16)
    h2 = h1_ref[...] + jnp.dot(m, wo_ref[...], preferred_element_type=F32)

    h3 = h2 + 0.5 * _swiglu(*_norm_matmul_operands(h2, n2_ref[...]), wgu_ref, wd_ref, act_ref)
    out_ref[...] = _rmsnorm(h3, nf_ref[...])


def _resident(shape):
    nd = len(shape)
    return pl.BlockSpec(shape, lambda *_: (0,) * nd, pipeline_mode=pl.Buffered(1))


def _layer(x2, batch, seq, layer, n1, own_weights, nm, pw, ps, late_weights, n2, nf):
    tokens, d_model = x2.shape
    d_ff = own_weights[1].shape[1]
    pool_width = late_weights[0].shape[1]
    sb_width = late_weights[1].shape[1]
    n_pairs = sb_width // LANES
    n_heads = 2 * n_pairs
    gate_width = 2 * d_model
    tm = TOKEN_TILE
    assert seq % tm == 0 and tm % HALO == 0 and d_ff % FF_CHUNK == 0 and gate_width % 512 == 0
    assert sb_width == N_SB_HEADS * SB_HEAD_DIM and max(POOL_WINDOWS) - 1 <= HALO
    n_tiles = tokens // tm
    tiles_per_seq = seq // tm
    params = functools.partial(pltpu.CompilerParams, vmem_limit_bytes=VMEM_LIMIT_BYTES)

    n_sl = CAST_SLICES
    assert n_tiles >= n_sl
    assert all(w.shape[1] % (n_sl * BF16_SUBLANES) == 0 for w in (*own_weights, *late_weights))
    tile = lambda s: jnp.maximum(s - n_sl, 0)
    row_tile = lambda width: pl.BlockSpec((tm, width), lambda s: (tile(s), 0))
    head_tiles = pl.BlockSpec((2 * n_heads, tm, LANES), lambda s: (0, tile(s), 0))
    own_slice = lambda s: jnp.minimum(s, n_sl - 1)
    late_slice = lambda s: jnp.clip(s - n_sl, 0, n_sl - 1)
    f32_slices = lambda w, which: pl.BlockSpec((None, w.shape[1] // n_sl, w.shape[2]),
                                               lambda s: (layer, which(s), 0))
    bf16_slices = [pl.BlockSpec((w.shape[1] // n_sl, w.shape[2]), lambda s: (late_slice(s), 0))
                   for w in late_weights]
    h1, xg, qv, kt, wbp, wba, wo, wgu2, wd2 = pl.pallas_call(
        functools.partial(_ffn_proj_kernel, pool_width=pool_width, sb_width=sb_width,
                          n_cast=len(late_weights), n_slices=n_sl),
        grid=(n_sl + n_tiles,),
        in_specs=[row_tile(d_model), _resident(n1.shape), _resident(nm.shape),
                  *[f32_slices(w, own_slice) for w in own_weights],
                  *[f32_slices(w, late_slice) for w in late_weights]],
        out_specs=[row_tile(d_model), row_tile(pool_width + gate_width), head_tiles,
                   pl.BlockSpec((n_pairs, 1, LANES, tm), lambda s: (0, tile(s), 0, 0)), *bf16_slices],
        out_shape=[jax.ShapeDtypeStruct((tokens, d_model), F32),
                   jax.ShapeDtypeStruct((tokens, pool_width + gate_width), F32),
                   jax.ShapeDtypeStruct((2 * n_heads, tokens, LANES), BF16),
                   jax.ShapeDtypeStruct((n_pairs, n_tiles, LANES, tm), BF16),
                   *[jax.ShapeDtypeStruct(w.shape[1:], BF16) for w in late_weights]],
        scratch_shapes=[pltpu.VMEM((tm, d_ff), BF16),
                        *[pltpu.VMEM(w.shape[1:], BF16) for w in own_weights]],
        compiler_params=params(dimension_semantics=("arbitrary",)),
        name="ffn1_proj",
    )(x2, n1, nm, *own_weights, *late_weights)

    q_tile = lambda width: pl.BlockSpec((tm, width), lambda b, i: (b * tiles_per_seq + i, 0))
    halo_blocks = tm // HALO
    out = pl.pallas_call(
        _mixer_ffn_kernel,
        grid=(batch, tiles_per_seq),
        in_specs=[q_tile(d_model),
                  pl.BlockSpec((2 * n_heads, tm, LANES), lambda b, i: (0, b * tiles_per_seq + i, 0)),
                  pl.BlockSpec((n_pairs, 1, LANES, tm), lambda b, i: (0, b * tiles_per_seq + i, 0, 0)),
                  q_tile(pool_width + gate_width),
                  pl.BlockSpec((HALO, pool_width),
                               lambda b, i: (jnp.maximum((b * tiles_per_seq + i) * halo_blocks - 1, 0), 0)),
                  _resident(pw.shape), _resident(ps.shape), _resident(wbp.shape), _resident(wba.shape),
                  _resident(wo.shape), _resident(n2.shape), _resident(wgu2.shape), _resident(wd2.shape),
                  _resident(nf.shape)],
        out_specs=q_tile(d_model),
        out_shape=jax.ShapeDtypeStruct((tokens, d_model), F32),
        scratch_shapes=[pltpu.VMEM((n_pairs, tiles_per_seq, LANES, tm), BF16),
                        pltpu.VMEM((n_heads, seq, LANES), BF16),
                        pltpu.VMEM((tm, sb_width), F32),
                        pltpu.VMEM((n_heads, tm, 1), F32),
                        pltpu.VMEM((2 * n_heads, tm, tm), F32),
                        pltpu.VMEM((2 * n_heads, tm, tm), BF16),
                        pltpu.VMEM((tm, pool_width), BF16),
                        pltpu.VMEM((tm, d_ff), BF16)],
        compiler_params=params(dimension_semantics=("arbitrary", "arbitrary")),
        name="mixer_ffn2",
    )(h1, qv, kt, xg, xg, pw, ps, wbp, wba, wo, n2, wgu2, wd2, nf)
    return out


def kernel(x, ffn1_norm, ffn1_w_gate_up, ffn1_w_down, mix_norm, w_in, pool_w_group, pool_scale,
           w_branch_pool, w_branch_attn, w_out, ffn2_norm, ffn2_w_gate_up, ffn2_w_down, final_norm):
    batch, seq, d_model = x.shape
    depth = ffn1_norm.shape[0]
    row = lambda a: a.reshape(1, -1)
    h = x.reshape(batch * seq, d_model)
    for l in range(depth):
        assert l == depth - 1
        h = _layer(h, batch, seq, l,
                   row(ffn1_norm[l]), (ffn1_w_gate_up, ffn1_w_down, w_in), row(mix_norm[l]),
                   pool_w_group[l].astype(BF16), row(pool_scale[l]),
                   (w_branch_pool, w_branch_attn, w_out, ffn2_w_gate_up, ffn2_w_down),
                   row(ffn2_norm[l]), row(final_norm))
    return h.reshape(batch, seq, d_model)
```

```python
import functools

import jax
import jax.numpy as jnp
from jax import lax
from jax.experimental import pallas as pl
from jax.experimental.pallas import tpu as pltpu

F32 = jnp.float32
BF16 = jnp.bfloat16

RMS_EPS = 1e-6
LOG2E = 1.4426950408889634
POOL_WINDOWS = (2, 4, 8, 16)
N_SB_HEADS = 8
SB_HEAD_DIM = 64
LANES = 128
BF16_SUBLANES = 16
HALO = 16
TOKEN_TILE = 256
FF_CHUNK = 256
SWEEP_BLOCKS = 3
CAST_SLICES = 16
VMEM_LIMIT_BYTES = 56 * 1024 * 1024


def _rms_scale(x):
    return lax.rsqrt(jnp.mean(x * x, axis=-1, keepdims=True) + RMS_EPS)


def _rmsnorm(x, g_row):
    return x * _rms_scale(x) * g_row


def _norm_matmul_operands(x, g_row):
    return (x * g_row).astype(BF16), _rms_scale(x)


def _swiglu(xg, r, wgu_ref, wd_ref, act_ref):
    d_ff = wd_ref.shape[0]
    for c0 in range(0, d_ff, FF_CHUNK):
        g = jnp.dot(xg, wgu_ref[:, c0:c0 + FF_CHUNK], preferred_element_type=F32) * r
        u = jnp.dot(xg, wgu_ref[:, d_ff + c0:d_ff + c0 + FF_CHUNK], preferred_element_type=F32) * r
        act_ref[:, c0:c0 + FF_CHUNK] = (g * jax.nn.sigmoid(g) * u).astype(BF16)
    return jnp.dot(act_ref[...], wd_ref[...], preferred_element_type=F32)


def _ffn_proj_kernel(x_ref, n1_ref, nm_ref, *rest, pool_width, sb_width, n_cast, n_slices):
    own_f32, rest = rest[:3], rest[3:]
    cast_in, rest = rest[:n_cast], rest[n_cast:]
    (h1_ref, xg_ref, qv_ref, kt_ref), rest = rest[:4], rest[4:]
    cast_out, rest = rest[:n_cast], rest[n_cast:]
    act_ref, wgu_ref, wd_ref, win_ref = rest
    xp_ref, gl_ref = xg_ref.at[:, 0:pool_width], xg_ref.at[:, pool_width:xg_ref.shape[1]]
    n_heads = qv_ref.shape[0] // 2
    q_ref, v_ref = qv_ref.at[0:n_heads], qv_ref.at[n_heads:2 * n_heads]
    step = pl.program_id(0)

    @pl.when(step < n_slices)
    def _():
        for src, dst in zip(own_f32, (wgu_ref, wd_ref, win_ref)):
            rows = src.shape[0]
            dst[pl.ds(pl.multiple_of(step * rows, rows), rows), :] = src[...].astype(BF16)

    @pl.when(step >= n_slices)
    def _():
        @pl.when(step < 2 * n_slices)
        def _():
            for src, dst in zip(cast_in, cast_out):
                dst[...] = src[...].astype(BF16)

        _ffn_proj_tile(x_ref, n1_ref, wgu_ref, wd_ref, nm_ref, win_ref,
                       h1_ref, xp_ref, q_ref, kt_ref, v_ref, gl_ref, act_ref, pool_width, sb_width)


def _ffn_proj_tile(x_ref, n1_ref, wgu_ref, wd_ref, nm_ref, win_ref,
                   h1_ref, xp_ref, q_ref, kt_ref, v_ref, gl_ref, act_ref, pool_width, sb_width):
    x = x_ref[...]
    h1 = x + 0.5 * _swiglu(*_norm_matmul_operands(x, n1_ref[...]), wgu_ref, wd_ref, act_ref)
    h1_ref[...] = h1
    ug, r = _norm_matmul_operands(h1, nm_ref[...])
    proj = lambda lo, width: jnp.dot(ug, win_ref[:, lo:lo + width], preferred_element_type=F32) * r

    o1 = pool_width
    xp_ref[...] = proj(0, o1)
    n_pairs = sb_width // LANES
    lane_lo = lax.broadcasted_iota(jnp.int32, (1, LANES), 1) < SB_HEAD_DIM

    def head_copies(ref, t):
        for p in range(n_pairs):
            tp = t[:, p * LANES:(p + 1) * LANES].astype(BF16)
            zero = jnp.zeros_like(tp)
            ref[2 * p] = jnp.where(lane_lo, tp, zero)
            ref[2 * p + 1] = jnp.where(lane_lo, zero, tp)

    head_copies(q_ref, proj(o1, sb_width) * (LOG2E / (SB_HEAD_DIM ** 0.5)))
    k = proj(o1 + sb_width, sb_width)
    for p in range(n_pairs):
        kt_ref[p, 0] = k[:, p * LANES:(p + 1) * LANES].T.astype(BF16)
    head_copies(v_ref, proj(o1 + 2 * sb_width, sb_width))
    o4 = o1 + 3 * sb_width
    gw = gl_ref.shape[1]
    for c0 in range(0, gw, 512):
        gl_ref[:, c0:c0 + 512] = proj(o4 + c0, 512)


def _attend(q_ref, kt_ref, v_ref, osb_ref, carry_ref, i):
    n_heads, tq, _ = q_ref.shape
    n_pairs = n_heads // 2
    tk = tq
    row = lax.broadcasted_iota(jnp.int32, (tq, tk), 0)
    col = lax.broadcasted_iota(jnp.int32, (tq, tk), 1)
    tri = (row > col).astype(BF16)
    diag_mask = col < row

    def log_terms(h, j, masked):
        z = jnp.dot(q_ref[h], kt_ref[h // 2, j], preferred_element_type=F32)
        log_beta = jnp.minimum(z, 0.0) - jnp.log(1.0 + jnp.exp2(-jnp.abs(z))) * LOG2E
        l = log_beta - z
        return log_beta, (jnp.where(diag_mask, l, 0.0) if masked else l)

    def row_sums(l):
        return jnp.broadcast_to(jnp.sum(l, axis=-1, keepdims=True), (tq, LANES))

    def widen(c):
        return jnp.concatenate([c] * (tk // LANES), axis=1)

    def reverse_cumsum(l16):
        return jnp.dot(l16, tri, preferred_element_type=F32)

    def weigh_values(p, js, a_cat, first):
        v_cat = [v_ref[h, pl.ds(pl.multiple_of(j * tk, tk), tk), :] for j in js for h in (2 * p, 2 * p + 1)]
        o = jnp.dot(jnp.concatenate(a_cat, axis=1), jnp.concatenate(v_cat, axis=0),
                    preferred_element_type=F32)
        lanes = slice(p * LANES, (p + 1) * LANES)
        if first:
            osb_ref[:, lanes] = o
        else:
            osb_ref[:, lanes] += o

    def sweep(js, from_diagonal):
        for p in range(n_pairs):
            a_cat = []
            for slot, j in enumerate(js):
                for h in (2 * p, 2 * p + 1):
                    diagonal = from_diagonal and slot == 0
                    log_beta, l = log_terms(h, j, masked=diagonal)
                    log_a = log_beta + reverse_cumsum(l.astype(BF16))
                    if diagonal:
                        a = jnp.where(diag_mask, jnp.exp2(log_a), 0.0)
                        carry_ref[h] = row_sums(l)
                    else:
                        a = jnp.exp2(log_a + widen(carry_ref[h]))
                        carry_ref[h] += row_sums(l)
                    a_cat.append(a.astype(BF16))
            weigh_values(p, js, a_cat, first=from_diagonal)

    n_first = lax.rem(i, SWEEP_BLOCKS) + 1
    for n in range(1, SWEEP_BLOCKS + 1):
        @pl.when(n_first == n)
        def _(n=n):
            sweep([i - d for d in range(n)], from_diagonal=True)

    first = i - n_first

    @pl.loop(0, lax.div(first + 1, SWEEP_BLOCKS))
    def _(t):
        sweep([first - SWEEP_BLOCKS * t - d for d in range(SWEEP_BLOCKS)], from_diagonal=False)


def _mixer_ffn_kernel(h1_ref, qv_ref, kt_new_ref, xg_ref, xph_ref,
                      pw_ref, ps_ref, wbp_ref, wba_ref, wo_ref, n2_ref, wgu_ref, wd_ref, nf_ref,
                      out_ref, kt_ref, v_ref, osb_ref, carry_ref, pm_ref, act_ref):
    i = pl.program_id(1)
    tq = h1_ref.shape[0]
    d_model = h1_ref.shape[1]
    pool_width = xph_ref.shape[1]
    xp_ref, gl_ref = xg_ref.at[:, 0:pool_width], xg_ref.at[:, pool_width:xg_ref.shape[1]]
    n_heads = qv_ref.shape[0] // 2
    q_ref, v_new_ref = qv_ref.at[0:n_heads], qv_ref.at[n_heads:2 * n_heads]

    for p in range(kt_ref.shape[0]):
        kt_ref[p, i] = kt_new_ref[p, 0]
    v_ref[:, pl.ds(pl.multiple_of(i * tq, tq), tq), :] = v_new_ref[...]

    _attend(q_ref, kt_ref, v_ref, osb_ref, carry_ref, i)

    halo = xph_ref[...] * (i > 0).astype(F32)
    pos = i * tq + lax.broadcasted_iota(jnp.int32, (tq, 1), 0)
    group = pw_ref.shape[1]
    for gi, w in enumerate(POOL_WINDOWS):
        lanes = slice(gi * group, (gi + 1) * group)
        xg = xp_ref[:, lanes]
        s = jnp.concatenate([halo[:, lanes], xg], axis=0)
        d = 1
        while d < w:
            s = s + pltpu.roll(s, d, 0)
            d *= 2
        count = jnp.minimum(pos + 1, w).astype(F32)
        y = s[HALO:, :] / count - xg
        yg = jnp.dot(y.astype(BF16), pw_ref[gi], preferred_element_type=F32)
        pm_ref[:, lanes] = (yg * ps_ref[:, lanes]).astype(BF16)

    y_pool = jnp.dot(pm_ref[...], wbp_ref[...], preferred_element_type=F32)
    y_sb = jnp.dot(osb_ref[...].astype(BF16), wba_ref[...], preferred_element_type=F32)
    g_pool = jax.nn.sigmoid(gl_ref[:, 0:d_model])
    g_sb = jax.nn.sigmoid(gl_ref[:, d_model:2 * d_model])
    m = (g_pool * y_pool + g_sb * y_sb).astype(BF16)
    h2 = h1_ref[...] + jnp.dot(m, wo_ref[...], preferred_element_type=F32)

    h3 = h2 + 0.5 * _swiglu(*_norm_matmul_operands(h2, n2_ref[...]), wgu_ref, wd_ref, act_ref)
    out_ref[...] = _rmsnorm(h3, nf_ref[...])


def _resident(shape):
    nd = len(shape)
    return pl.BlockSpec(shape, lambda *_: (0,) * nd, pipeline_mode=pl.Buffered(1))


def _layer(x2, batch, seq, layer, n1, own_weights, nm, pw, ps, late_weights, n2, nf):
    tokens, d_model = x2.shape
    d_ff = own_weights[1].shape[1]
    pool_width = late_weights[0].shape[1]
    sb_width = late_weights[1].shape[1]
    n_pairs = sb_width // LANES
    n_heads = 2 * n_pairs
    gate_width = 2 * d_model
    tm = TOKEN_TILE
    assert seq % tm == 0 and tm % HALO == 0 and d_ff % FF_CHUNK == 0 and gate_width % 512 == 0
    assert sb_width == N_SB_HEADS * SB_HEAD_DIM and max(POOL_WINDOWS) - 1 <= HALO
    n_tiles = tokens // tm
    tiles_per_seq = seq // tm
    params = functools.partial(pltpu.CompilerParams, vmem_limit_bytes=VMEM_LIMIT_BYTES)

    n_sl = CAST_SLICES
    assert n_tiles >= n_sl
    assert all(w.shape[1] % (n_sl * BF16_SUBLANES) == 0 for w in (*own_weights, *late_weights))
    tile = lambda s: jnp.maximum(s - n_sl, 0)
    row_tile = lambda width: pl.BlockSpec((tm, width), lambda s: (tile(s), 0))
    head_tiles = pl.BlockSpec((2 * n_heads, tm, LANES), lambda s: (0, tile(s), 0))
    own_slice = lambda s: jnp.minimum(s, n_sl - 1)
    late_slice = lambda s: jnp.clip(s - n_sl, 0, n_sl - 1)
    f32_slices = lambda w, which: pl.BlockSpec((None, w.shape[1] // n_sl, w.shape[2]),
                                               lambda s: (layer, which(s), 0))
    bf16_slices = [pl.BlockSpec((w.shape[1] // n_sl, w.shape[2]), lambda s: (late_slice(s), 0))
                   for w in late_weights]
    h1, xg, qv, kt, wbp, wba, wo, wgu2, wd2 = pl.pallas_call(
        functools.partial(_ffn_proj_kernel, pool_width=pool_width, sb_width=sb_width,
                          n_cast=len(late_weights), n_slices=n_sl),
        grid=(n_sl + n_tiles,),
        in_specs=[row_tile(d_model), _resident(n1.shape), _resident(nm.shape),
                  *[f32_slices(w, own_slice) for w in own_weights],
                  *[f32_slices(w, late_slice) for w in late_weights]],
        out_specs=[row_tile(d_model), row_tile(pool_width + gate_width), head_tiles,
                   pl.BlockSpec((n_pairs, 1, LANES, tm), lambda s: (0, tile(s), 0, 0)), *bf16_slices],
        out_shape=[jax.ShapeDtypeStruct((tokens, d_model), F32),
                   jax.ShapeDtypeStruct((tokens, pool_width + gate_width), F32),
                   jax.ShapeDtypeStruct((2 * n_heads, tokens, LANES), BF16),
                   jax.ShapeDtypeStruct((n_pairs, n_tiles, LANES, tm), BF16),
                   *[jax.ShapeDtypeStruct(w.shape[1:], BF16) for w in late_weights]],
        scratch_shapes=[pltpu.VMEM((tm, d_ff), BF16),
                        *[pltpu.VMEM(w.shape[1:], BF16) for w in own_weights]],
        compiler_params=params(dimension_semantics=("arbitrary",)),
        name="ffn1_proj",
    )(x2, n1, nm, *own_weights, *late_weights)

    q_tile = lambda width: pl.BlockSpec((tm, width), lambda b, i: (b * tiles_per_seq + i, 0))
    halo_blocks = tm // HALO
    out = pl.pallas_call(
        _mixer_ffn_kernel,
        grid=(batch, tiles_per_seq),
        in_specs=[q_tile(d_model),
                  pl.BlockSpec((2 * n_heads, tm, LANES), lambda b, i: (0, b * tiles_per_seq + i, 0)),
                  pl.BlockSpec((n_pairs, 1, LANES, tm), lambda b, i: (0, b * tiles_per_seq + i, 0, 0)),
                  q_tile(pool_width + gate_width),
                  pl.BlockSpec((HALO, pool_width),
                               lambda b, i: (jnp.maximum((b * tiles_per_seq + i) * halo_blocks - 1, 0), 0)),
                  _resident(pw.shape), _resident(ps.shape), _resident(wbp.shape), _resident(wba.shape),
                  _resident(wo.shape), _resident(n2.shape), _resident(wgu2.shape), _resident(wd2.shape),
                  _resident(nf.shape)],
        out_specs=q_tile(d_model),
        out_shape=jax.ShapeDtypeStruct((tokens, d_model), F32),
        scratch_shapes=[pltpu.VMEM((n_pairs, tiles_per_seq, LANES, tm), BF16),
                        pltpu.VMEM((n_heads, seq, LANES), BF16),
                        pltpu.VMEM((tm, sb_width), F32),
                        pltpu.VMEM((n_heads, tm, LANES), F32),
                        pltpu.VMEM((tm, pool_width), BF16),
                        pltpu.VMEM((tm, d_ff), BF16)],
        compiler_params=params(dimension_semantics=("arbitrary", "arbitrary")),
        name="mixer_ffn2",
    )(h1, qv, kt, xg, xg, pw, ps, wbp, wba, wo, n2, wgu2, wd2, nf)
    return out


def kernel(x, ffn1_norm, ffn1_w_gate_up, ffn1_w_down, mix_norm, w_in, pool_w_group, pool_scale,
           w_branch_pool, w_branch_attn, w_out, ffn2_norm, ffn2_w_gate_up, ffn2_w_down, final_norm):
    batch, seq, d_model = x.shape
    depth = ffn1_norm.shape[0]
    row = lambda a: a.reshape(1, -1)
    h = x.reshape(batch * seq, d_model)
    for l in range(depth):
        assert l == depth - 1
        h = _layer(h, batch, seq, l,
                   row(ffn1_norm[l]), (ffn1_w_gate_up, ffn1_w_down, w_in), row(mix_norm[l]),
                   pool_w_group[l].astype(BF16), row(pool_scale[l]),
                   (w_branch_pool, w_branch_attn, w_out, ffn2_w_gate_up, ffn2_w_down),
                   row(ffn2_norm[l]), row(final_norm))
    return h.reshape(batch, seq, d_model)
```

```python
import functools

import jax
import jax.numpy as jnp
from jax import lax
from jax.experimental import pallas as pl
from jax.experimental.pallas import tpu as pltpu

F32 = jnp.float32
BF16 = jnp.bfloat16

RMS_EPS = 1e-6
LOG2E = 1.4426950408889634
POOL_WINDOWS = (2, 4, 8, 16)
N_SB_HEADS = 8
SB_HEAD_DIM = 64
LANES = 128
BF16_SUBLANES = 16
HALO = 16
TOKEN_TILE = 256
FF_CHUNK = 256
SWEEP_BLOCKS = 4
CAST_SLICES = 16
VMEM_LIMIT_BYTES = 56 * 1024 * 1024


def _rms_scale(x):
    return lax.rsqrt(jnp.mean(x * x, axis=-1, keepdims=True) + RMS_EPS)


def _rmsnorm(x, g_row):
    return x * _rms_scale(x) * g_row


def _norm_matmul_operands(x, g_row):
    return (x * g_row).astype(BF16), _rms_scale(x)


def _swiglu(xg, r, wgu_ref, wd_ref, act_ref):
    d_ff = wd_ref.shape[0]
    for c0 in range(0, d_ff, FF_CHUNK):
        g = jnp.dot(xg, wgu_ref[:, c0:c0 + FF_CHUNK], preferred_element_type=F32) * r
        u = jnp.dot(xg, wgu_ref[:, d_ff + c0:d_ff + c0 + FF_CHUNK], preferred_element_type=F32) * r
        act_ref[:, c0:c0 + FF_CHUNK] = (g * jax.nn.sigmoid(g) * u).astype(BF16)
    return jnp.dot(act_ref[...], wd_ref[...], preferred_element_type=F32)


def _ffn_proj_kernel(x_ref, n1_ref, nm_ref, *rest, pool_width, sb_width, n_cast, n_slices):
    own_f32, rest = rest[:3], rest[3:]
    cast_in, rest = rest[:n_cast], rest[n_cast:]
    (h1_ref, xg_ref, qv_ref, kt_ref), rest = rest[:4], rest[4:]
    cast_out, rest = rest[:n_cast], rest[n_cast:]
    act_ref, wgu_ref, wd_ref, win_ref = rest
    xp_ref, gl_ref = xg_ref.at[:, 0:pool_width], xg_ref.at[:, pool_width:xg_ref.shape[1]]
    n_heads = qv_ref.shape[0] // 2
    q_ref, v_ref = qv_ref.at[0:n_heads], qv_ref.at[n_heads:2 * n_heads]
    step = pl.program_id(0)

    @pl.when(step < n_slices)
    def _():
        for src, dst in zip(own_f32, (wgu_ref, wd_ref, win_ref)):
            rows = src.shape[0]
            dst[pl.ds(pl.multiple_of(step * rows, rows), rows), :] = src[...].astype(BF16)

    @pl.when(step >= n_slices)
    def _():
        @pl.when(step < 2 * n_slices)
        def _():
            for src, dst in zip(cast_in, cast_out):
                dst[...] = src[...].astype(BF16)

        _ffn_proj_tile(x_ref, n1_ref, wgu_ref, wd_ref, nm_ref, win_ref,
                       h1_ref, xp_ref, q_ref, kt_ref, v_ref, gl_ref, act_ref, pool_width, sb_width)


def _ffn_proj_tile(x_ref, n1_ref, wgu_ref, wd_ref, nm_ref, win_ref,
                   h1_ref, xp_ref, q_ref, kt_ref, v_ref, gl_ref, act_ref, pool_width, sb_width):
    x = x_ref[...]
    h1 = x + 0.5 * _swiglu(*_norm_matmul_operands(x, n1_ref[...]), wgu_ref, wd_ref, act_ref)
    h1_ref[...] = h1
    ug, r = _norm_matmul_operands(h1, nm_ref[...])
    proj = lambda lo, width: jnp.dot(ug, win_ref[:, lo:lo + width], preferred_element_type=F32) * r

    o1 = pool_width
    xp_ref[...] = proj(0, o1)
    n_pairs = sb_width // LANES
    lane_lo = lax.broadcasted_iota(jnp.int32, (1, LANES), 1) < SB_HEAD_DIM

    def head_copies(ref, t):
        for p in range(n_pairs):
            tp = t[:, p * LANES:(p + 1) * LANES].astype(BF16)
            zero = jnp.zeros_like(tp)
            ref[2 * p] = jnp.where(lane_lo, tp, zero)
            ref[2 * p + 1] = jnp.where(lane_lo, zero, tp)

    head_copies(q_ref, proj(o1, sb_width) * (LOG2E / (SB_HEAD_DIM ** 0.5)))
    k = proj(o1 + sb_width, sb_width)
    for p in range(n_pairs):
        kt_ref[p, 0] = k[:, p * LANES:(p + 1) * LANES].T.astype(BF16)
    head_copies(v_ref, proj(o1 + 2 * sb_width, sb_width))
    o4 = o1 + 3 * sb_width
    gw = gl_ref.shape[1]
    for c0 in range(0, gw, 512):
        gl_ref[:, c0:c0 + 512] = proj(o4 + c0, 512)


def _attend(q_ref, kt_ref, v_ref, osb_ref, carry_ref, i):
    n_heads, tq, _ = q_ref.shape
    n_pairs = n_heads // 2
    tk = tq
    row = lax.broadcasted_iota(jnp.int32, (tq, tk), 0)
    col = lax.broadcasted_iota(jnp.int32, (tq, tk), 1)
    tri = (row > col).astype(BF16)
    diag_mask = col < row

    def log_terms(h, j, masked):
        z = jnp.dot(q_ref[h], kt_ref[h // 2, j], preferred_element_type=F32)
        log_beta = jnp.minimum(z, 0.0) - jnp.log(1.0 + jnp.exp2(-jnp.abs(z))) * LOG2E
        l = log_beta - z
        return log_beta, (jnp.where(diag_mask, l, 0.0) if masked else l)

    def row_sums(l):
        return jnp.broadcast_to(jnp.sum(l, axis=-1, keepdims=True), (tq, LANES))

    def widen(c):
        return jnp.concatenate([c] * (tk // LANES), axis=1)

    def reverse_cumsum(l16):
        return jnp.dot(l16, tri, preferred_element_type=F32)

    def weigh_values(p, js, a_cat, first):
        v_cat = [v_ref[h, pl.ds(pl.multiple_of(j * tk, tk), tk), :] for j in js for h in (2 * p, 2 * p + 1)]
        o = jnp.dot(jnp.concatenate(a_cat, axis=1), jnp.concatenate(v_cat, axis=0),
                    preferred_element_type=F32)
        lanes = slice(p * LANES, (p + 1) * LANES)
        if first:
            osb_ref[:, lanes] = o
        else:
            osb_ref[:, lanes] += o

    def sweep(js, from_diagonal):
        for p in range(n_pairs):
            a_cat = []
            for slot, j in enumerate(js):
                for h in (2 * p, 2 * p + 1):
                    diagonal = from_diagonal and slot == 0
                    log_beta, l = log_terms(h, j, masked=diagonal)
                    log_a = log_beta + reverse_cumsum(l.astype(BF16))
                    if diagonal:
                        a = jnp.where(diag_mask, jnp.exp2(log_a), 0.0)
                        carry_ref[h] = row_sums(l)
                    else:
                        a = jnp.exp2(log_a + widen(carry_ref[h]))
                        carry_ref[h] += row_sums(l)
                    a_cat.append(a.astype(BF16))
            weigh_values(p, js, a_cat, first=from_diagonal)

    n_first = lax.rem(i, SWEEP_BLOCKS) + 1
    for n in range(1, SWEEP_BLOCKS + 1):
        @pl.when(n_first == n)
        def _(n=n):
            sweep([i - d for d in range(n)], from_diagonal=True)

    first = i - n_first

    @pl.loop(0, lax.div(first + 1, SWEEP_BLOCKS))
    def _(t):
        sweep([first - SWEEP_BLOCKS * t - d for d in range(SWEEP_BLOCKS)], from_diagonal=False)


def _mixer_ffn_kernel(h1_ref, qv_ref, kt_new_ref, xg_ref, xph_ref,
                      pw_ref, ps_ref, wbp_ref, wba_ref, wo_ref, n2_ref, wgu_ref, wd_ref, nf_ref,
                      out_ref, kt_ref, v_ref, osb_ref, carry_ref, pm_ref, act_ref):
    i = pl.program_id(1)
    tq = h1_ref.shape[0]
    d_model = h1_ref.shape[1]
    pool_width = xph_ref.shape[1]
    xp_ref, gl_ref = xg_ref.at[:, 0:pool_width], xg_ref.at[:, pool_width:xg_ref.shape[1]]
    n_heads = qv_ref.shape[0] // 2
    q_ref, v_new_ref = qv_ref.at[0:n_heads], qv_ref.at[n_heads:2 * n_heads]

    for p in range(kt_ref.shape[0]):
        kt_ref[p, i] = kt_new_ref[p, 0]
    v_ref[:, pl.ds(pl.multiple_of(i * tq, tq), tq), :] = v_new_ref[...]

    _attend(q_ref, kt_ref, v_ref, osb_ref, carry_ref, i)

    halo = xph_ref[...] * (i > 0).astype(F32)
    pos = i * tq + lax.broadcasted_iota(jnp.int32, (tq, 1), 0)
    group = pw_ref.shape[1]
    for gi, w in enumerate(POOL_WINDOWS):
        lanes = slice(gi * group, (gi + 1) * group)
        xg = xp_ref[:, lanes]
        s = jnp.concatenate([halo[:, lanes], xg], axis=0)
        d = 1
        while d < w:
            s = s + pltpu.roll(s, d, 0)
            d *= 2
        count = jnp.minimum(pos + 1, w).astype(F32)
        y = s[HALO:, :] / count - xg
        yg = jnp.dot(y.astype(BF16), pw_ref[gi], preferred_element_type=F32)
        pm_ref[:, lanes] = (yg * ps_ref[:, lanes]).astype(BF16)

    y_pool = jnp.dot(pm_ref[...], wbp_ref[...], preferred_element_type=F32)
    y_sb = jnp.dot(osb_ref[...].astype(BF16), wba_ref[...], preferred_element_type=F32)
    g_pool = jax.nn.sigmoid(gl_ref[:, 0:d_model])
    g_sb = jax.nn.sigmoid(gl_ref[:, d_model:2 * d_model])
    m = (g_pool * y_pool + g_sb * y_sb).astype(BF16)
    h2 = h1_ref[...] + jnp.dot(m, wo_ref[...], preferred_element_type=F32)

    h3 = h2 + 0.5 * _swiglu(*_norm_matmul_operands(h2, n2_ref[...]), wgu_ref, wd_ref, act_ref)
    out_ref[...] = _rmsnorm(h3, nf_ref[...])


def _resident(shape):
    nd = len(shape)
    return pl.BlockSpec(shape, lambda *_: (0,) * nd, pipeline_mode=pl.Buffered(1))


def _layer(x2, batch, seq, layer, n1, own_weights, nm, pw, ps, late_weights, n2, nf):
    tokens, d_model = x2.shape
    d_ff = own_weights[1].shape[1]
    pool_width = late_weights[0].shape[1]
    sb_width = late_weights[1].shape[1]
    n_pairs = sb_width // LANES
    n_heads = 2 * n_pairs
    gate_width = 2 * d_model
    tm = TOKEN_TILE
    assert seq % tm == 0 and tm % HALO == 0 and d_ff % FF_CHUNK == 0 and gate_width % 512 == 0
    assert sb_width == N_SB_HEADS * SB_HEAD_DIM and max(POOL_WINDOWS) - 1 <= HALO
    n_tiles = tokens // tm
    tiles_per_seq = seq // tm
    params = functools.partial(pltpu.CompilerParams, vmem_limit_bytes=VMEM_LIMIT_BYTES)

    n_sl = CAST_SLICES
    assert n_tiles >= n_sl
    assert all(w.shape[1] % (n_sl * BF16_SUBLANES) == 0 for w in (*own_weights, *late_weights))
    tile = lambda s: jnp.maximum(s - n_sl, 0)
    row_tile = lambda width: pl.BlockSpec((tm, width), lambda s: (tile(s), 0))
    head_tiles = pl.BlockSpec((2 * n_heads, tm, LANES), lambda s: (0, tile(s), 0))
    own_slice = lambda s: jnp.minimum(s, n_sl - 1)
    late_slice = lambda s: jnp.clip(s - n_sl, 0, n_sl - 1)
    f32_slices = lambda w, which: pl.BlockSpec((None, w.shape[1] // n_sl, w.shape[2]),
                                               lambda s: (layer, which(s), 0))
    bf16_slices = [pl.BlockSpec((w.shape[1] // n_sl, w.shape[2]), lambda s: (late_slice(s), 0))
                   for w in late_weights]
    h1, xg, qv, kt, wbp, wba, wo, wgu2, wd2 = pl.pallas_call(
        functools.partial(_ffn_proj_kernel, pool_width=pool_width, sb_width=sb_width,
                          n_cast=len(late_weights), n_slices=n_sl),
        grid=(n_sl + n_tiles,),
        in_specs=[row_tile(d_model), _resident(n1.shape), _resident(nm.shape),
                  *[f32_slices(w, own_slice) for w in own_weights],
                  *[f32_slices(w, late_slice) for w in late_weights]],
        out_specs=[row_tile(d_model), row_tile(pool_width + gate_width), head_tiles,
                   pl.BlockSpec((n_pairs, 1, LANES, tm), lambda s: (0, tile(s), 0, 0)), *bf16_slices],
        out_shape=[jax.ShapeDtypeStruct((tokens, d_model), F32),
                   jax.ShapeDtypeStruct((tokens, pool_width + gate_width), F32),
                   jax.ShapeDtypeStruct((2 * n_heads, tokens, LANES), BF16),
                   jax.ShapeDtypeStruct((n_pairs, n_tiles, LANES, tm), BF16),
                   *[jax.ShapeDtypeStruct(w.shape[1:], BF16) for w in late_weights]],
        scratch_shapes=[pltpu.VMEM((tm, d_ff), BF16),
                        *[pltpu.VMEM(w.shape[1:], BF16) for w in own_weights]],
        compiler_params=params(dimension_semantics=("arbitrary",)),
        name="ffn1_proj",
    )(x2, n1, nm, *own_weights, *late_weights)

    q_tile = lambda width: pl.BlockSpec((tm, width), lambda b, i: (b * tiles_per_seq + i, 0))
    halo_blocks = tm // HALO
    out = pl.pallas_call(
        _mixer_ffn_kernel,
        grid=(batch, tiles_per_seq),
        in_specs=[q_tile(d_model),
                  pl.BlockSpec((2 * n_heads, tm, LANES), lambda b, i: (0, b * tiles_per_seq + i, 0)),
                  pl.BlockSpec((n_pairs, 1, LANES, tm), lambda b, i: (0, b * tiles_per_seq + i, 0, 0)),
                  q_tile(pool_width + gate_width),
                  pl.BlockSpec((HALO, pool_width),
                               lambda b, i: (jnp.maximum((b * tiles_per_seq + i) * halo_blocks - 1, 0), 0)),
                  _resident(pw.shape), _resident(ps.shape), _resident(wbp.shape), _resident(wba.shape),
                  _resident(wo.shape), _resident(n2.shape), _resident(wgu2.shape), _resident(wd2.shape),
                  _resident(nf.shape)],
        out_specs=q_tile(d_model),
        out_shape=jax.ShapeDtypeStruct((tokens, d_model), F32),
        scratch_shapes=[pltpu.VMEM((n_pairs, tiles_per_seq, LANES, tm), BF16),
                        pltpu.VMEM((n_heads, seq, LANES), BF16),
                        pltpu.VMEM((tm, sb_width), F32),
                        pltpu.VMEM((n_heads, tm, LANES), F32),
                        pltpu.VMEM((tm, pool_width), BF16),
                        pltpu.VMEM((tm, d_ff), BF16)],
        compiler_params=params(dimension_semantics=("arbitrary", "arbitrary")),
        name="mixer_ffn2",
    )(h1, qv, kt, xg, xg, pw, ps, wbp, wba, wo, n2, wgu2, wd2, nf)
    return out


def kernel(x, ffn1_norm, ffn1_w_gate_up, ffn1_w_down, mix_norm, w_in, pool_w_group, pool_scale,
           w_branch_pool, w_branch_attn, w_out, ffn2_norm, ffn2_w_gate_up, ffn2_w_down, final_norm):
    batch, seq, d_model = x.shape
    depth = ffn1_norm.shape[0]
    row = lambda a: a.reshape(1, -1)
    h = x.reshape(batch * seq, d_model)
    for l in range(depth):
        assert l == depth - 1
        h = _layer(h, batch, seq, l,
                   row(ffn1_norm[l]), (ffn1_w_gate_up, ffn1_w_down, w_in), row(mix_norm[l]),
                   pool_w_group[l].astype(BF16), row(pool_scale[l]),
                   (w_branch_pool, w_branch_attn, w_out, ffn2_w_gate_up, ffn2_w_down),
                   row(ffn2_norm[l]), row(final_norm))
    return h.reshape(batch, seq, d_model)
```

```python
import functools

import jax
import jax.numpy as jnp
from jax import lax
from jax.experimental import pallas as pl
from jax.experimental.pallas import tpu as pltpu

F32 = jnp.float32
BF16 = jnp.bfloat16

RMS_EPS = 1e-6
LOG2E = 1.4426950408889634
POOL_WINDOWS = (2, 4, 8, 16)
N_SB_HEADS = 8
SB_HEAD_DIM = 64
LANES = 128
BF16_SUBLANES = 16
HALO = 16
TOKEN_TILE = 256
FF_CHUNK = 256
SWEEP_BLOCKS = 8
CAST_SLICES = 16
VMEM_LIMIT_BYTES = 56 * 1024 * 1024


def _rms_scale(x):
    return lax.rsqrt(jnp.mean(x * x, axis=-1, keepdims=True) + RMS_EPS)


def _rmsnorm(x, g_row):
    return x * _rms_scale(x) * g_row


def _norm_matmul_operands(x, g_row):
    return (x * g_row).astype(BF16), _rms_scale(x)


def _swiglu(xg, r, wgu_ref, wd_ref, act_ref):
    d_ff = wd_ref.shape[0]
    for c0 in range(0, d_ff, FF_CHUNK):
        g = jnp.dot(xg, wgu_ref[:, c0:c0 + FF_CHUNK], preferred_element_type=F32) * r
        u = jnp.dot(xg, wgu_ref[:, d_ff + c0:d_ff + c0 + FF_CHUNK], preferred_element_type=F32) * r
        act_ref[:, c0:c0 + FF_CHUNK] = (g * jax.nn.sigmoid(g) * u).astype(BF16)
    return jnp.dot(act_ref[...], wd_ref[...], preferred_element_type=F32)


def _ffn_proj_kernel(x_ref, n1_ref, nm_ref, *rest, pool_width, sb_width, n_cast, n_slices):
    own_f32, rest = rest[:3], rest[3:]
    cast_in, rest = rest[:n_cast], rest[n_cast:]
    (h1_ref, xg_ref, qv_ref, kt_ref), rest = rest[:4], rest[4:]
    cast_out, rest = rest[:n_cast], rest[n_cast:]
    act_ref, wgu_ref, wd_ref, win_ref = rest
    xp_ref, gl_ref = xg_ref.at[:, 0:pool_width], xg_ref.at[:, pool_width:xg_ref.shape[1]]
    n_heads = qv_ref.shape[0] // 2
    q_ref, v_ref = qv_ref.at[0:n_heads], qv_ref.at[n_heads:2 * n_heads]
    step = pl.program_id(0)

    @pl.when(step < n_slices)
    def _():
        for src, dst in zip(own_f32, (wgu_ref, wd_ref, win_ref)):
            rows = src.shape[0]
            dst[pl.ds(pl.multiple_of(step * rows, rows), rows), :] = src[...].astype(BF16)

    @pl.when(step >= n_slices)
    def _():
        @pl.when(step < 2 * n_slices)
        def _():
            for src, dst in zip(cast_in, cast_out):
                dst[...] = src[...].astype(BF16)

        _ffn_proj_tile(x_ref, n1_ref, wgu_ref, wd_ref, nm_ref, win_ref,
                       h1_ref, xp_ref, q_ref, kt_ref, v_ref, gl_ref, act_ref, pool_width, sb_width)


def _ffn_proj_tile(x_ref, n1_ref, wgu_ref, wd_ref, nm_ref, win_ref,
                   h1_ref, xp_ref, q_ref, kt_ref, v_ref, gl_ref, act_ref, pool_width, sb_width):
    x = x_ref[...]
    h1 = x + 0.5 * _swiglu(*_norm_matmul_operands(x, n1_ref[...]), wgu_ref, wd_ref, act_ref)
    h1_ref[...] = h1
    ug, r = _norm_matmul_operands(h1, nm_ref[...])
    proj = lambda lo, width: jnp.dot(ug, win_ref[:, lo:lo + width], preferred_element_type=F32) * r

    o1 = pool_width
    xp_ref[...] = proj(0, o1)
    n_pairs = sb_width // LANES
    lane_lo = lax.broadcasted_iota(jnp.int32, (1, LANES), 1) < SB_HEAD_DIM

    def head_copies(ref, t):
        for p in range(n_pairs):
            tp = t[:, p * LANES:(p + 1) * LANES].astype(BF16)
            zero = jnp.zeros_like(tp)
            ref[2 * p] = jnp.where(lane_lo, tp, zero)
            ref[2 * p + 1] = jnp.where(lane_lo, zero, tp)

    head_copies(q_ref, proj(o1, sb_width) * (LOG2E / (SB_HEAD_DIM ** 0.5)))
    k = proj(o1 + sb_width, sb_width)
    for p in range(n_pairs):
        kt_ref[p, 0] = k[:, p * LANES:(p + 1) * LANES].T.astype(BF16)
    head_copies(v_ref, proj(o1 + 2 * sb_width, sb_width))
    o4 = o1 + 3 * sb_width
    gw = gl_ref.shape[1]
    for c0 in range(0, gw, 512):
        gl_ref[:, c0:c0 + 512] = proj(o4 + c0, 512)


def _attend(q_ref, kt_ref, v_ref, osb_ref, carry_ref, i):
    n_heads, tq, _ = q_ref.shape
    n_pairs = n_heads // 2
    tk = tq
    row = lax.broadcasted_iota(jnp.int32, (tq, tk), 0)
    col = lax.broadcasted_iota(jnp.int32, (tq, tk), 1)
    tri = (row > col).astype(BF16)
    diag_mask = col < row

    def log_terms(h, j, masked):
        z = jnp.dot(q_ref[h], kt_ref[h // 2, j], preferred_element_type=F32)
        log_beta = jnp.minimum(z, 0.0) - jnp.log(1.0 + jnp.exp2(-jnp.abs(z))) * LOG2E
        l = log_beta - z
        return log_beta, (jnp.where(diag_mask, l, 0.0) if masked else l)

    def row_sums(l):
        return jnp.broadcast_to(jnp.sum(l, axis=-1, keepdims=True), (tq, LANES))

    def widen(c):
        return jnp.concatenate([c] * (tk // LANES), axis=1)

    def reverse_cumsum(l16):
        return jnp.dot(l16, tri, preferred_element_type=F32)

    def weigh_values(p, js, a_cat, first):
        v_cat = [v_ref[h, pl.ds(pl.multiple_of(j * tk, tk), tk), :] for j in js for h in (2 * p, 2 * p + 1)]
        o = jnp.dot(jnp.concatenate(a_cat, axis=1), jnp.concatenate(v_cat, axis=0),
                    preferred_element_type=F32)
        lanes = slice(p * LANES, (p + 1) * LANES)
        if first:
            osb_ref[:, lanes] = o
        else:
            osb_ref[:, lanes] += o

    def sweep(js, from_diagonal):
        for p in range(n_pairs):
            a_cat = []
            for slot, j in enumerate(js):
                for h in (2 * p, 2 * p + 1):
                    diagonal = from_diagonal and slot == 0
                    log_beta, l = log_terms(h, j, masked=diagonal)
                    log_a = log_beta + reverse_cumsum(l.astype(BF16))
                    if diagonal:
                        a = jnp.where(diag_mask, jnp.exp2(log_a), 0.0)
                        carry_ref[h] = row_sums(l)
                    else:
                        a = jnp.exp2(log_a + widen(carry_ref[h]))
                        carry_ref[h] += row_sums(l)
                    a_cat.append(a.astype(BF16))
            weigh_values(p, js, a_cat, first=from_diagonal)

    sweep_blocks = min(SWEEP_BLOCKS, kt_ref.shape[1])
    n_first = lax.rem(i, sweep_blocks) + 1
    for n in range(1, sweep_blocks + 1):
        @pl.when(n_first == n)
        def _(n=n):
            sweep([i - d for d in range(n)], from_diagonal=True)

    if sweep_blocks < kt_ref.shape[1]:
        first = i - n_first

        @pl.loop(0, lax.div(first + 1, sweep_blocks))
        def _(t):
            sweep([first - sweep_blocks * t - d for d in range(sweep_blocks)], from_diagonal=False)


def _mixer_ffn_kernel(h1_ref, qv_ref, kt_new_ref, xg_ref, xph_ref,
                      pw_ref, ps_ref, wbp_ref, wba_ref, wo_ref, n2_ref, wgu_ref, wd_ref, nf_ref,
                      out_ref, kt_ref, v_ref, osb_ref, carry_ref, pm_ref, act_ref):
    i = pl.program_id(1)
    tq = h1_ref.shape[0]
    d_model = h1_ref.shape[1]
    pool_width = xph_ref.shape[1]
    xp_ref, gl_ref = xg_ref.at[:, 0:pool_width], xg_ref.at[:, pool_width:xg_ref.shape[1]]
    n_heads = qv_ref.shape[0] // 2
    q_ref, v_new_ref = qv_ref.at[0:n_heads], qv_ref.at[n_heads:2 * n_heads]

    for p in range(kt_ref.shape[0]):
        kt_ref[p, i] = kt_new_ref[p, 0]
    v_ref[:, pl.ds(pl.multiple_of(i * tq, tq), tq), :] = v_new_ref[...]

    _attend(q_ref, kt_ref, v_ref, osb_ref, carry_ref, i)

    halo = xph_ref[...] * (i > 0).astype(F32)
    pos = i * tq + lax.broadcasted_iota(jnp.int32, (tq, 1), 0)
    group = pw_ref.shape[1]
    for gi, w in enumerate(POOL_WINDOWS):
        lanes = slice(gi * group, (gi + 1) * group)
        xg = xp_ref[:, lanes]
        s = jnp.concatenate([halo[:, lanes], xg], axis=0)
        d = 1
        while d < w:
            s = s + pltpu.roll(s, d, 0)
            d *= 2
        count = jnp.minimum(pos + 1, w).astype(F32)
        y = s[HALO:, :] / count - xg
        yg = jnp.dot(y.astype(BF16), pw_ref[gi], preferred_element_type=F32)
        pm_ref[:, lanes] = (yg * ps_ref[:, lanes]).astype(BF16)

    y_pool = jnp.dot(pm_ref[...], wbp_ref[...], preferred_element_type=F32)
    y_sb = jnp.dot(osb_ref[...].astype(BF16), wba_ref[...], preferred_element_type=F32)
    g_pool = jax.nn.sigmoid(gl_ref[:, 0:d_model])
    g_sb = jax.nn.sigmoid(gl_ref[:, d_model:2 * d_model])
    m = (g_pool * y_pool + g_sb * y_sb).astype(BF16)
    h2 = h1_ref[...] + jnp.dot(m, wo_ref[...], preferred_element_type=F32)

    h3 = h2 + 0.5 * _swiglu(*_norm_matmul_operands(h2, n2_ref[...]), wgu_ref, wd_ref, act_ref)
    out_ref[...] = _rmsnorm(h3, nf_ref[...])


def _resident(shape):
    nd = len(shape)
    return pl.BlockSpec(shape, lambda *_: (0,) * nd, pipeline_mode=pl.Buffered(1))


def _layer(x2, batch, seq, layer, n1, own_weights, nm, pw, ps, late_weights, n2, nf):
    tokens, d_model = x2.shape
    d_ff = own_weights[1].shape[1]
    pool_width = late_weights[0].shape[1]
    sb_width = late_weights[1].shape[1]
    n_pairs = sb_width // LANES
    n_heads = 2 * n_pairs
    gate_width = 2 * d_model
    tm = TOKEN_TILE
    assert seq % tm == 0 and tm % HALO == 0 and d_ff % FF_CHUNK == 0 and gate_width % 512 == 0
    assert sb_width == N_SB_HEADS * SB_HEAD_DIM and max(POOL_WINDOWS) - 1 <= HALO
    n_tiles = tokens // tm
    tiles_per_seq = seq // tm
    params = functools.partial(pltpu.CompilerParams, vmem_limit_bytes=VMEM_LIMIT_BYTES)

    n_sl = CAST_SLICES
    assert n_tiles >= n_sl
    assert all(w.shape[1] % (n_sl * BF16_SUBLANES) == 0 for w in (*own_weights, *late_weights))
    tile = lambda s: jnp.maximum(s - n_sl, 0)
    row_tile = lambda width: pl.BlockSpec((tm, width), lambda s: (tile(s), 0))
    head_tiles = pl.BlockSpec((2 * n_heads, tm, LANES), lambda s: (0, tile(s), 0))
    own_slice = lambda s: jnp.minimum(s, n_sl - 1)
    late_slice = lambda s: jnp.clip(s - n_sl, 0, n_sl - 1)
    f32_slices = lambda w, which: pl.BlockSpec((None, w.shape[1] // n_sl, w.shape[2]),
                                               lambda s: (layer, which(s), 0))
    bf16_slices = [pl.BlockSpec((w.shape[1] // n_sl, w.shape[2]), lambda s: (late_slice(s), 0))
                   for w in late_weights]
    h1, xg, qv, kt, wbp, wba, wo, wgu2, wd2 = pl.pallas_call(
        functools.partial(_ffn_proj_kernel, pool_width=pool_width, sb_width=sb_width,
                          n_cast=len(late_weights), n_slices=n_sl),
        grid=(n_sl + n_tiles,),
        in_specs=[row_tile(d_model), _resident(n1.shape), _resident(nm.shape),
                  *[f32_slices(w, own_slice) for w in own_weights],
                  *[f32_slices(w, late_slice) for w in late_weights]],
        out_specs=[row_tile(d_model), row_tile(pool_width + gate_width), head_tiles,
                   pl.BlockSpec((n_pairs, 1, LANES, tm), lambda s: (0, tile(s), 0, 0)), *bf16_slices],
        out_shape=[jax.ShapeDtypeStruct((tokens, d_model), F32),
                   jax.ShapeDtypeStruct((tokens, pool_width + gate_width), F32),
                   jax.ShapeDtypeStruct((2 * n_heads, tokens, LANES), BF16),
                   jax.ShapeDtypeStruct((n_pairs, n_tiles, LANES, tm), BF16),
                   *[jax.ShapeDtypeStruct(w.shape[1:], BF16) for w in late_weights]],
        scratch_shapes=[pltpu.VMEM((tm, d_ff), BF16),
                        *[pltpu.VMEM(w.shape[1:], BF16) for w in own_weights]],
        compiler_params=params(dimension_semantics=("arbitrary",)),
        name="ffn1_proj",
    )(x2, n1, nm, *own_weights, *late_weights)

    q_tile = lambda width: pl.BlockSpec((tm, width), lambda b, i: (b * tiles_per_seq + i, 0))
    halo_blocks = tm // HALO
    out = pl.pallas_call(
        _mixer_ffn_kernel,
        grid=(batch, tiles_per_seq),
        in_specs=[q_tile(d_model),
                  pl.BlockSpec((2 * n_heads, tm, LANES), lambda b, i: (0, b * tiles_per_seq + i, 0)),
                  pl.BlockSpec((n_pairs, 1, LANES, tm), lambda b, i: (0, b * tiles_per_seq + i, 0, 0)),
                  q_tile(pool_width + gate_width),
                  pl.BlockSpec((HALO, pool_width),
                               lambda b, i: (jnp.maximum((b * tiles_per_seq + i) * halo_blocks - 1, 0), 0)),
                  _resident(pw.shape), _resident(ps.shape), _resident(wbp.shape), _resident(wba.shape),
                  _resident(wo.shape), _resident(n2.shape), _resident(wgu2.shape), _resident(wd2.shape),
                  _resident(nf.shape)],
        out_specs=q_tile(d_model),
        out_shape=jax.ShapeDtypeStruct((tokens, d_model), F32),
        scratch_shapes=[pltpu.VMEM((n_pairs, tiles_per_seq, LANES, tm), BF16),
                        pltpu.VMEM((n_heads, seq, LANES), BF16),
                        pltpu.VMEM((tm, sb_width), F32),
                        pltpu.VMEM((n_heads, tm, LANES), F32),
                        pltpu.VMEM((tm, pool_width), BF16),
                        pltpu.VMEM((tm, d_ff), BF16)],
        compiler_params=params(dimension_semantics=("arbitrary", "arbitrary")),
        name="mixer_ffn2",
    )(h1, qv, kt, xg, xg, pw, ps, wbp, wba, wo, n2, wgu2, wd2, nf)
    return out


def kernel(x, ffn1_norm, ffn1_w_gate_up, ffn1_w_down, mix_norm, w_in, pool_w_group, pool_scale,
           w_branch_pool, w_branch_attn, w_out, ffn2_norm, ffn2_w_gate_up, ffn2_w_down, final_norm):
    batch, seq, d_model = x.shape
    depth = ffn1_norm.shape[0]
    row = lambda a: a.reshape(1, -1)
    h = x.reshape(batch * seq, d_model)
    for l in range(depth):
        assert l == depth - 1
        h = _layer(h, batch, seq, l,
                   row(ffn1_norm[l]), (ffn1_w_gate_up, ffn1_w_down, w_in), row(mix_norm[l]),
                   pool_w_group[l].astype(BF16), row(pool_scale[l]),
                   (w_branch_pool, w_branch_attn, w_out, ffn2_w_gate_up, ffn2_w_down),
                   row(ffn2_norm[l]), row(final_norm))
    return h.reshape(batch, seq, d_model)
```

```python
import functools

import jax
import jax.numpy as jnp
from jax import lax
from jax.experimental import pallas as pl
from jax.experimental.pallas import tpu as pltpu

F32 = jnp.float32
BF16 = jnp.bfloat16

RMS_EPS = 1e-6
LOG2E = 1.4426950408889634
POOL_WINDOWS = (2, 4, 8, 16)
N_SB_HEADS = 8
SB_HEAD_DIM = 64
LANES = 128
BF16_SUBLANES = 16
HALO = 16
TOKEN_TILE = 256
FF_CHUNK = 256
SWEEP_BLOCKS = 4
CAST_SLICES = 16
VMEM_LIMIT_BYTES = 56 * 1024 * 1024


def _rms_scale(x):
    return lax.rsqrt(jnp.mean(x * x, axis=-1, keepdims=True) + RMS_EPS)


def _rmsnorm(x, g_row):
    return x * _rms_scale(x) * g_row


def _norm_matmul_operands(x, g_row):
    return (x * g_row).astype(BF16), _rms_scale(x)


def _swiglu(xg, r, wgu_ref, wd_ref, act_ref):
    d_ff = wd_ref.shape[0]
    for c0 in range(0, d_ff, FF_CHUNK):
        g = jnp.dot(xg, wgu_ref[:, c0:c0 + FF_CHUNK], preferred_element_type=F32) * r
        u = jnp.dot(xg, wgu_ref[:, d_ff + c0:d_ff + c0 + FF_CHUNK], preferred_element_type=F32) * r
        act_ref[:, c0:c0 + FF_CHUNK] = (g * jax.nn.sigmoid(g) * u).astype(BF16)
    return jnp.dot(act_ref[...], wd_ref[...], preferred_element_type=F32)


def _ffn_proj_kernel(x_ref, n1_ref, nm_ref, *rest, pool_width, sb_width, n_cast, n_slices):
    own_f32, rest = rest[:3], rest[3:]
    cast_in, rest = rest[:n_cast], rest[n_cast:]
    (h1_ref, xg_ref, qv_ref, kt_ref), rest = rest[:4], rest[4:]
    cast_out, rest = rest[:n_cast], rest[n_cast:]
    act_ref, wgu_ref, wd_ref, win_ref = rest
    xp_ref, gl_ref = xg_ref.at[:, 0:pool_width], xg_ref.at[:, pool_width:xg_ref.shape[1]]
    n_heads = qv_ref.shape[0] // 2
    q_ref, v_ref = qv_ref.at[0:n_heads], qv_ref.at[n_heads:2 * n_heads]
    step = pl.program_id(0)

    @pl.when(step < n_slices)
    def _():
        for src, dst in zip(own_f32, (wgu_ref, wd_ref, win_ref)):
            rows = src.shape[0]
            dst[pl.ds(pl.multiple_of(step * rows, rows), rows), :] = src[...].astype(BF16)

    @pl.when(step >= n_slices)
    def _():
        @pl.when(step < 2 * n_slices)
        def _():
            for src, dst in zip(cast_in, cast_out):
                dst[...] = src[...].astype(BF16)

        _ffn_proj_tile(x_ref, n1_ref, wgu_ref, wd_ref, nm_ref, win_ref,
                       h1_ref, xp_ref, q_ref, kt_ref, v_ref, gl_ref, act_ref, pool_width, sb_width)


def _ffn_proj_tile(x_ref, n1_ref, wgu_ref, wd_ref, nm_ref, win_ref,
                   h1_ref, xp_ref, q_ref, kt_ref, v_ref, gl_ref, act_ref, pool_width, sb_width):
    x = x_ref[...]
    h1 = x + 0.5 * _swiglu(*_norm_matmul_operands(x, n1_ref[...]), wgu_ref, wd_ref, act_ref)
    h1_ref[...] = h1
    ug, r = _norm_matmul_operands(h1, nm_ref[...])
    proj = lambda lo, width: jnp.dot(ug, win_ref[:, lo:lo + width], preferred_element_type=F32) * r

    o1 = pool_width
    xp_ref[...] = proj(0, o1)
    n_pairs = sb_width // LANES
    lane_lo = lax.broadcasted_iota(jnp.int32, (1, LANES), 1) < SB_HEAD_DIM

    def head_copies(ref, t):
        for p in range(n_pairs):
            tp = t[:, p * LANES:(p + 1) * LANES].astype(BF16)
            zero = jnp.zeros_like(tp)
            ref[2 * p] = jnp.where(lane_lo, tp, zero)
            ref[2 * p + 1] = jnp.where(lane_lo, zero, tp)

    head_copies(q_ref, proj(o1, sb_width) * (LOG2E / (SB_HEAD_DIM ** 0.5)))
    k = proj(o1 + sb_width, sb_width)
    for p in range(n_pairs):
        kt_ref[p, 0] = k[:, p * LANES:(p + 1) * LANES].T.astype(BF16)
    head_copies(v_ref, proj(o1 + 2 * sb_width, sb_width))
    o4 = o1 + 3 * sb_width
    gw = gl_ref.shape[1]
    for c0 in range(0, gw, 512):
        gl_ref[:, c0:c0 + 512] = proj(o4 + c0, 512)


def _attend(q_ref, kt_ref, v_ref, osb_ref, carry_ref, i):
    n_heads, tq, _ = q_ref.shape
    n_pairs = n_heads // 2
    tk = tq
    row = lax.broadcasted_iota(jnp.int32, (tq, tk), 0)
    col = lax.broadcasted_iota(jnp.int32, (tq, tk), 1)
    tri = (row > col).astype(BF16)
    diag_mask = col < row
    lane_lo = lax.broadcasted_iota(jnp.int32, (1, LANES), 1) < SB_HEAD_DIM

    def log_terms(h, j, masked):
        z = jnp.dot(q_ref[h], kt_ref[h // 2, j], preferred_element_type=F32)
        log_beta = jnp.minimum(z, 0.0) - jnp.log(1.0 + jnp.exp2(-jnp.abs(z))) * LOG2E
        l = log_beta - z
        return log_beta, (jnp.where(diag_mask, l, 0.0) if masked else l)

    def row_sums(l):
        return jnp.broadcast_to(jnp.sum(l, axis=-1, keepdims=True), (tq, LANES))

    def widen(c):
        return jnp.concatenate([c] * (tk // LANES), axis=1)

    def reverse_cumsum(l16):
        return jnp.dot(l16, tri, preferred_element_type=F32)

    def sweep(js, from_diagonal):
        for p in range(n_pairs):
            pair = (2 * p, 2 * p + 1)
            a_cat = [None] * (2 * len(js))
            totals = []
            for hh, h in enumerate(pair):
                local = None
                for slot, j in enumerate(js):
                    diagonal = from_diagonal and slot == 0
                    log_beta, l = log_terms(h, j, masked=diagonal)
                    log_a = log_beta + reverse_cumsum(l.astype(BF16))
                    if slot > 0:
                        log_a = log_a + widen(local)
                    a = jnp.exp2(log_a)
                    if diagonal:
                        a = jnp.where(diag_mask, a, 0.0)
                    local = row_sums(l) if slot == 0 else local + row_sums(l)
                    a_cat[2 * slot + hh] = a.astype(BF16)
                totals.append(local)
            v_cat = [v_ref[h, pl.ds(pl.multiple_of(j * tk, tk), tk), :] for j in js for h in pair]
            o = jnp.dot(jnp.concatenate(a_cat, axis=1), jnp.concatenate(v_cat, axis=0),
                        preferred_element_type=F32)
            lanes = slice(p * LANES, (p + 1) * LANES)
            if from_diagonal:
                osb_ref[:, lanes] = o
                for h, total in zip(pair, totals):
                    carry_ref[h] = total
            else:
                osb_ref[:, lanes] += o * jnp.where(lane_lo, jnp.exp2(carry_ref[pair[0]]),
                                                   jnp.exp2(carry_ref[pair[1]]))
                for h, total in zip(pair, totals):
                    carry_ref[h] += total

    n_first = lax.rem(i, SWEEP_BLOCKS) + 1
    for n in range(1, SWEEP_BLOCKS + 1):
        @pl.when(n_first == n)
        def _(n=n):
            sweep([i - d for d in range(n)], from_diagonal=True)

    first = i - n_first

    @pl.loop(0, lax.div(first + 1, SWEEP_BLOCKS))
    def _(t):
        sweep([first - SWEEP_BLOCKS * t - d for d in range(SWEEP_BLOCKS)], from_diagonal=False)


def _mixer_ffn_kernel(h1_ref, qv_ref, kt_new_ref, xg_ref, xph_ref,
                      pw_ref, ps_ref, wbp_ref, wba_ref, wo_ref, n2_ref, wgu_ref, wd_ref, nf_ref,
                      out_ref, kt_ref, v_ref, osb_ref, carry_ref, pm_ref, act_ref):
    i = pl.program_id(1)
    tq = h1_ref.shape[0]
    d_model = h1_ref.shape[1]
    pool_width = xph_ref.shape[1]
    xp_ref, gl_ref = xg_ref.at[:, 0:pool_width], xg_ref.at[:, pool_width:xg_ref.shape[1]]
    n_heads = qv_ref.shape[0] // 2
    q_ref, v_new_ref = qv_ref.at[0:n_heads], qv_ref.at[n_heads:2 * n_heads]

    for p in range(kt_ref.shape[0]):
        kt_ref[p, i] = kt_new_ref[p, 0]
    v_ref[:, pl.ds(pl.multiple_of(i * tq, tq), tq), :] = v_new_ref[...]

    _attend(q_ref, kt_ref, v_ref, osb_ref, carry_ref, i)

    halo = xph_ref[...] * (i > 0).astype(F32)
    pos = i * tq + lax.broadcasted_iota(jnp.int32, (tq, 1), 0)
    group = pw_ref.shape[1]
    for gi, w in enumerate(POOL_WINDOWS):
        lanes = slice(gi * group, (gi + 1) * group)
        xg = xp_ref[:, lanes]
        s = jnp.concatenate([halo[:, lanes], xg], axis=0)
        d = 1
        while d < w:
            s = s + pltpu.roll(s, d, 0)
            d *= 2
        count = jnp.minimum(pos + 1, w).astype(F32)
        y = s[HALO:, :] / count - xg
        yg = jnp.dot(y.astype(BF16), pw_ref[gi], preferred_element_type=F32)
        pm_ref[:, lanes] = (yg * ps_ref[:, lanes]).astype(BF16)

    y_pool = jnp.dot(pm_ref[...], wbp_ref[...], preferred_element_type=F32)
    y_sb = jnp.dot(osb_ref[...].astype(BF16), wba_ref[...], preferred_element_type=F32)
    g_pool = jax.nn.sigmoid(gl_ref[:, 0:d_model])
    g_sb = jax.nn.sigmoid(gl_ref[:, d_model:2 * d_model])
    m = (g_pool * y_pool + g_sb * y_sb).astype(BF16)
    h2 = h1_ref[...] + jnp.dot(m, wo_ref[...], preferred_element_type=F32)

    h3 = h2 + 0.5 * _swiglu(*_norm_matmul_operands(h2, n2_ref[...]), wgu_ref, wd_ref, act_ref)
    out_ref[...] = _rmsnorm(h3, nf_ref[...])


def _resident(shape):
    nd = len(shape)
    return pl.BlockSpec(shape, lambda *_: (0,) * nd, pipeline_mode=pl.Buffered(1))


def _layer(x2, batch, seq, layer, n1, own_weights, nm, pw, ps, late_weights, n2, nf):
    tokens, d_model = x2.shape
    d_ff = own_weights[1].shape[1]
    pool_width = late_weights[0].shape[1]
    sb_width = late_weights[1].shape[1]
    n_pairs = sb_width // LANES
    n_heads = 2 * n_pairs
    gate_width = 2 * d_model
    tm = TOKEN_TILE
    assert seq % tm == 0 and tm % HALO == 0 and d_ff % FF_CHUNK == 0 and gate_width % 512 == 0
    assert sb_width == N_SB_HEADS * SB_HEAD_DIM and max(POOL_WINDOWS) - 1 <= HALO
    n_tiles = tokens // tm
    tiles_per_seq = seq // tm
    params = functools.partial(pltpu.CompilerParams, vmem_limit_bytes=VMEM_LIMIT_BYTES)

    n_sl = CAST_SLICES
    assert n_tiles >= n_sl
    assert all(w.shape[1] % (n_sl * BF16_SUBLANES) == 0 for w in (*own_weights, *late_weights))
    tile = lambda s: jnp.maximum(s - n_sl, 0)
    row_tile = lambda width: pl.BlockSpec((tm, width), lambda s: (tile(s), 0))
    head_tiles = pl.BlockSpec((2 * n_heads, tm, LANES), lambda s: (0, tile(s), 0))
    own_slice = lambda s: jnp.minimum(s, n_sl - 1)
    late_slice = lambda s: jnp.clip(s - n_sl, 0, n_sl - 1)
    f32_slices = lambda w, which: pl.BlockSpec((None, w.shape[1] // n_sl, w.shape[2]),
                                               lambda s: (layer, which(s), 0))
    bf16_slices = [pl.BlockSpec((w.shape[1] // n_sl, w.shape[2]), lambda s: (late_slice(s), 0))
                   for w in late_weights]
    h1, xg, qv, kt, wbp, wba, wo, wgu2, wd2 = pl.pallas_call(
        functools.partial(_ffn_proj_kernel, pool_width=pool_width, sb_width=sb_width,
                          n_cast=len(late_weights), n_slices=n_sl),
        grid=(n_sl + n_tiles,),
        in_specs=[row_tile(d_model), _resident(n1.shape), _resident(nm.shape),
                  *[f32_slices(w, own_slice) for w in own_weights],
                  *[f32_slices(w, late_slice) for w in late_weights]],
        out_specs=[row_tile(d_model), row_tile(pool_width + gate_width), head_tiles,
                   pl.BlockSpec((n_pairs, 1, LANES, tm), lambda s: (0, tile(s), 0, 0)), *bf16_slices],
        out_shape=[jax.ShapeDtypeStruct((tokens, d_model), F32),
                   jax.ShapeDtypeStruct((tokens, pool_width + gate_width), F32),
                   jax.ShapeDtypeStruct((2 * n_heads, tokens, LANES), BF16),
                   jax.ShapeDtypeStruct((n_pairs, n_tiles, LANES, tm), BF16),
                   *[jax.ShapeDtypeStruct(w.shape[1:], BF16) for w in late_weights]],
        scratch_shapes=[pltpu.VMEM((tm, d_ff), BF16),
                        *[pltpu.VMEM(w.shape[1:], BF16) for w in own_weights]],
        compiler_params=params(dimension_semantics=("arbitrary",)),
        name="ffn1_proj",
    )(x2, n1, nm, *own_weights, *late_weights)

    q_tile = lambda width: pl.BlockSpec((tm, width), lambda b, i: (b * tiles_per_seq + i, 0))
    halo_blocks = tm // HALO
    out = pl.pallas_call(
        _mixer_ffn_kernel,
        grid=(batch, tiles_per_seq),
        in_specs=[q_tile(d_model),
                  pl.BlockSpec((2 * n_heads, tm, LANES), lambda b, i: (0, b * tiles_per_seq + i, 0)),
                  pl.BlockSpec((n_pairs, 1, LANES, tm), lambda b, i: (0, b * tiles_per_seq + i, 0, 0)),
                  q_tile(pool_width + gate_width),
                  pl.BlockSpec((HALO, pool_width),
                               lambda b, i: (jnp.maximum((b * tiles_per_seq + i) * halo_blocks - 1, 0), 0)),
                  _resident(pw.shape), _resident(ps.shape), _resident(wbp.shape), _resident(wba.shape),
                  _resident(wo.shape), _resident(n2.shape), _resident(wgu2.shape), _resident(wd2.shape),
                  _resident(nf.shape)],
        out_specs=q_tile(d_model),
        out_shape=jax.ShapeDtypeStruct((tokens, d_model), F32),
        scratch_shapes=[pltpu.VMEM((n_pairs, tiles_per_seq, LANES, tm), BF16),
                        pltpu.VMEM((n_heads, seq, LANES), BF16),
                        pltpu.VMEM((tm, sb_width), F32),
                        pltpu.VMEM((n_heads, tm, LANES), F32),
                        pltpu.VMEM((tm, pool_width), BF16),
                        pltpu.VMEM((tm, d_ff), BF16)],
        compiler_params=params(dimension_semantics=("arbitrary", "arbitrary")),
        name="mixer_ffn2",
    )(h1, qv, kt, xg, xg, pw, ps, wbp, wba, wo, n2, wgu2, wd2, nf)
    return out


def kernel(x, ffn1_norm, ffn1_w_gate_up, ffn1_w_down, mix_norm, w_in, pool_w_group, pool_scale,
           w_branch_pool, w_branch_attn, w_out, ffn2_norm, ffn2_w_gate_up, ffn2_w_down, final_norm):
    batch, seq, d_model = x.shape
    depth = ffn1_norm.shape[0]
    row = lambda a: a.reshape(1, -1)
    h = x.reshape(batch * seq, d_model)
    for l in range(depth):
        assert l == depth - 1
        h = _layer(h, batch, seq, l,
                   row(ffn1_norm[l]), (ffn1_w_gate_up, ffn1_w_down, w_in), row(mix_norm[l]),
                   pool_w_group[l].astype(BF16), row(pool_scale[l]),
                   (w_branch_pool, w_branch_attn, w_out, ffn2_w_gate_up, ffn2_w_down),
                   row(ffn2_norm[l]), row(final_norm))
    return h.reshape(batch, seq, d_model)
```

```python
import functools

import jax
import jax.numpy as jnp
from jax import lax
from jax.experimental import pallas as pl
from jax.experimental.pallas import tpu as pltpu

F32 = jnp.float32
BF16 = jnp.bfloat16

RMS_EPS = 1e-6
LOG2E = 1.4426950408889634
POOL_WINDOWS = (2, 4, 8, 16)
N_SB_HEADS = 8
SB_HEAD_DIM = 64
LANES = 128
BF16_SUBLANES = 16
HALO = 16
TOKEN_TILE = 256
FF_CHUNK = 256
SWEEP_BLOCKS = 4
CAST_SLICES = 16
VMEM_LIMIT_BYTES = 56 * 1024 * 1024


def _sigmoid(x):
    return 0.5 * jnp.tanh(0.5 * x) + 0.5


def _rms_scale(x):
    return lax.rsqrt(jnp.mean(x * x, axis=-1, keepdims=True) + RMS_EPS)


def _rmsnorm(x, g_row):
    return x * _rms_scale(x) * g_row


def _norm_matmul_operands(x, g_row):
    return (x * g_row).astype(BF16), _rms_scale(x)


def _swiglu(xg, r, wgu_ref, wd_ref, act_ref):
    d_ff = wd_ref.shape[0]
    for c0 in range(0, d_ff, FF_CHUNK):
        g = jnp.dot(xg, wgu_ref[:, c0:c0 + FF_CHUNK], preferred_element_type=F32) * r
        u = jnp.dot(xg, wgu_ref[:, d_ff + c0:d_ff + c0 + FF_CHUNK], preferred_element_type=F32) * r
        act_ref[:, c0:c0 + FF_CHUNK] = (g * _sigmoid(g) * u).astype(BF16)
    return jnp.dot(act_ref[...], wd_ref[...], preferred_element_type=F32)


def _ffn_proj_kernel(x_ref, n1_ref, nm_ref, *rest, pool_width, sb_width, n_cast, n_slices):
    own_f32, rest = rest[:3], rest[3:]
    cast_in, rest = rest[:n_cast], rest[n_cast:]
    (h1_ref, xg_ref, qv_ref, kt_ref), rest = rest[:4], rest[4:]
    cast_out, rest = rest[:n_cast], rest[n_cast:]
    act_ref, wgu_ref, wd_ref, win_ref = rest
    xp_ref, gl_ref = xg_ref.at[:, 0:pool_width], xg_ref.at[:, pool_width:xg_ref.shape[1]]
    n_heads = qv_ref.shape[0] // 2
    q_ref, v_ref = qv_ref.at[0:n_heads], qv_ref.at[n_heads:2 * n_heads]
    step = pl.program_id(0)

    @pl.when(step < n_slices)
    def _():
        for src, dst in zip(own_f32, (wgu_ref, wd_ref, win_ref)):
            rows = src.shape[0]
            dst[pl.ds(pl.multiple_of(step * rows, rows), rows), :] = src[...].astype(BF16)

    @pl.when(step >= n_slices)
    def _():
        @pl.when(step < 2 * n_slices)
        def _():
            for src, dst in zip(cast_in, cast_out):
                dst[...] = src[...].astype(BF16)

        _ffn_proj_tile(x_ref, n1_ref, wgu_ref, wd_ref, nm_ref, win_ref,
                       h1_ref, xp_ref, q_ref, kt_ref, v_ref, gl_ref, act_ref, pool_width, sb_width)


def _ffn_proj_tile(x_ref, n1_ref, wgu_ref, wd_ref, nm_ref, win_ref,
                   h1_ref, xp_ref, q_ref, kt_ref, v_ref, gl_ref, act_ref, pool_width, sb_width):
    x = x_ref[...]
    h1 = x + 0.5 * _swiglu(*_norm_matmul_operands(x, n1_ref[...]), wgu_ref, wd_ref, act_ref)
    h1_ref[...] = h1
    ug, r = _norm_matmul_operands(h1, nm_ref[...])
    proj = lambda lo, width: jnp.dot(ug, win_ref[:, lo:lo + width], preferred_element_type=F32) * r

    o1 = pool_width
    xp_ref[...] = proj(0, o1)
    n_pairs = sb_width // LANES
    lane_lo = lax.broadcasted_iota(jnp.int32, (1, LANES), 1) < SB_HEAD_DIM

    def head_copies(ref, t):
        for p in range(n_pairs):
            tp = t[:, p * LANES:(p + 1) * LANES].astype(BF16)
            zero = jnp.zeros_like(tp)
            ref[2 * p] = jnp.where(lane_lo, tp, zero)
            ref[2 * p + 1] = jnp.where(lane_lo, zero, tp)

    head_copies(q_ref, proj(o1, sb_width) * (LOG2E / (SB_HEAD_DIM ** 0.5)))
    k = proj(o1 + sb_width, sb_width)
    for p in range(n_pairs):
        kt_ref[p, 0] = k[:, p * LANES:(p + 1) * LANES].T.astype(BF16)
    head_copies(v_ref, proj(o1 + 2 * sb_width, sb_width))
    o4 = o1 + 3 * sb_width
    gw = gl_ref.shape[1]
    for c0 in range(0, gw, 512):
        gl_ref[:, c0:c0 + 512] = proj(o4 + c0, 512)


def _attend(q_ref, kt_ref, v_ref, osb_ref, carry_ref, i):
    n_heads, tq, _ = q_ref.shape
    n_pairs = n_heads // 2
    tk = tq
    row = lax.broadcasted_iota(jnp.int32, (tq, tk), 0)
    col = lax.broadcasted_iota(jnp.int32, (tq, tk), 1)
    tri = (row > col).astype(BF16)
    diag_mask = col < row
    lane_lo = lax.broadcasted_iota(jnp.int32, (1, LANES), 1) < SB_HEAD_DIM

    def log_terms(h, j, masked):
        z = jnp.dot(q_ref[h], kt_ref[h // 2, j], preferred_element_type=F32)
        log_beta = jnp.minimum(z, 0.0) - jnp.log(1.0 + jnp.exp2(-jnp.abs(z))) * LOG2E
        l = log_beta - z
        return log_beta, (jnp.where(diag_mask, l, 0.0) if masked else l)

    def row_sums(l):
        return jnp.broadcast_to(jnp.sum(l, axis=-1, keepdims=True), (tq, LANES))

    def widen(c):
        return jnp.concatenate([c] * (tk // LANES), axis=1)

    def reverse_cumsum(l16):
        return jnp.dot(l16, tri, preferred_element_type=F32)

    def sweep(js, from_diagonal):
        for p in range(n_pairs):
            pair = (2 * p, 2 * p + 1)
            a_cat = [None] * (2 * len(js))
            totals = []
            for hh, h in enumerate(pair):
                local = None
                for slot, j in enumerate(js):
                    diagonal = from_diagonal and slot == 0
                    log_beta, l = log_terms(h, j, masked=diagonal)
                    log_a = log_beta + reverse_cumsum(l.astype(BF16))
                    if slot > 0:
                        log_a = log_a + widen(local)
                    a = jnp.exp2(log_a)
                    if diagonal:
                        a = jnp.where(diag_mask, a, 0.0)
                    local = row_sums(l) if slot == 0 else local + row_sums(l)
                    a_cat[2 * slot + hh] = a.astype(BF16)
                totals.append(local)
            v_cat = [v_ref[h, pl.ds(pl.multiple_of(j * tk, tk), tk), :] for j in js for h in pair]
            o = jnp.dot(jnp.concatenate(a_cat, axis=1), jnp.concatenate(v_cat, axis=0),
                        preferred_element_type=F32)
            lanes = slice(p * LANES, (p + 1) * LANES)
            if from_diagonal:
                osb_ref[:, lanes] = o
                for h, total in zip(pair, totals):
                    carry_ref[h] = total
            else:
                osb_ref[:, lanes] += o * jnp.where(lane_lo, jnp.exp2(carry_ref[pair[0]]),
                                                   jnp.exp2(carry_ref[pair[1]]))
                for h, total in zip(pair, totals):
                    carry_ref[h] += total

    n_first = lax.rem(i, SWEEP_BLOCKS) + 1
    for n in range(1, SWEEP_BLOCKS + 1):
        @pl.when(n_first == n)
        def _(n=n):
            sweep([i - d for d in range(n)], from_diagonal=True)

    first = i - n_first

    @pl.loop(0, lax.div(first + 1, SWEEP_BLOCKS))
    def _(t):
        sweep([first - SWEEP_BLOCKS * t - d for d in range(SWEEP_BLOCKS)], from_diagonal=False)


def _mixer_ffn_kernel(h1_ref, qv_ref, kt_new_ref, xg_ref, xph_ref,
                      pw_ref, ps_ref, wbp_ref, wba_ref, wo_ref, n2_ref, wgu_ref, wd_ref, nf_ref,
                      out_ref, kt_ref, v_ref, osb_ref, carry_ref, pm_ref, act_ref):
    i = pl.program_id(1)
    tq = h1_ref.shape[0]
    d_model = h1_ref.shape[1]
    pool_width = xph_ref.shape[1]
    xp_ref, gl_ref = xg_ref.at[:, 0:pool_width], xg_ref.at[:, pool_width:xg_ref.shape[1]]
    n_heads = qv_ref.shape[0] // 2
    q_ref, v_new_ref = qv_ref.at[0:n_heads], qv_ref.at[n_heads:2 * n_heads]

    for p in range(kt_ref.shape[0]):
        kt_ref[p, i] = kt_new_ref[p, 0]
    v_ref[:, pl.ds(pl.multiple_of(i * tq, tq), tq), :] = v_new_ref[...]

    _attend(q_ref, kt_ref, v_ref, osb_ref, carry_ref, i)

    halo = xph_ref[...] * (i > 0).astype(F32)
    pos = i * tq + lax.broadcasted_iota(jnp.int32, (tq, 1), 0)
    group = pw_ref.shape[1]
    for gi, w in enumerate(POOL_WINDOWS):
        lanes = slice(gi * group, (gi + 1) * group)
        xg = xp_ref[:, lanes]
        s = jnp.concatenate([halo[:, lanes], xg], axis=0)
        d = 1
        while d < w:
            s = s + pltpu.roll(s, d, 0)
            d *= 2
        count = jnp.minimum(pos + 1, w).astype(F32)
        y = s[HALO:, :] / count - xg
        yg = jnp.dot(y.astype(BF16), pw_ref[gi], preferred_element_type=F32)
        pm_ref[:, lanes] = (yg * ps_ref[:, lanes]).astype(BF16)

    y_pool = jnp.dot(pm_ref[...], wbp_ref[...], preferred_element_type=F32)
    y_sb = jnp.dot(osb_ref[...].astype(BF16), wba_ref[...], preferred_element_type=F32)
    g_pool = _sigmoid(gl_ref[:, 0:d_model])
    g_sb = _sigmoid(gl_ref[:, d_model:2 * d_model])
    m = (g_pool * y_pool + g_sb * y_sb).astype(BF16)
    h2 = h1_ref[...] + jnp.dot(m, wo_ref[...], preferred_element_type=F32)

    h3 = h2 + 0.5 * _swiglu(*_norm_matmul_operands(h2, n2_ref[...]), wgu_ref, wd_ref, act_ref)
    out_ref[...] = _rmsnorm(h3, nf_ref[...])


def _resident(shape):
    nd = len(shape)
    return pl.BlockSpec(shape, lambda *_: (0,) * nd, pipeline_mode=pl.Buffered(1))


def _layer(x2, batch, seq, layer, n1, own_weights, nm, pw, ps, late_weights, n2, nf):
    tokens, d_model = x2.shape
    d_ff = own_weights[1].shape[1]
    pool_width = late_weights[0].shape[1]
    sb_width = late_weights[1].shape[1]
    n_pairs = sb_width // LANES
    n_heads = 2 * n_pairs
    gate_width = 2 * d_model
    tm = TOKEN_TILE
    assert seq % tm == 0 and tm % HALO == 0 and d_ff % FF_CHUNK == 0 and gate_width % 512 == 0
    assert sb_width == N_SB_HEADS * SB_HEAD_DIM and max(POOL_WINDOWS) - 1 <= HALO
    n_tiles = tokens // tm
    tiles_per_seq = seq // tm
    params = functools.partial(pltpu.CompilerParams, vmem_limit_bytes=VMEM_LIMIT_BYTES)

    n_sl = CAST_SLICES
    assert n_tiles >= n_sl
    assert all(w.shape[1] % (n_sl * BF16_SUBLANES) == 0 for w in (*own_weights, *late_weights))
    tile = lambda s: jnp.maximum(s - n_sl, 0)
    row_tile = lambda width: pl.BlockSpec((tm, width), lambda s: (tile(s), 0))
    head_tiles = pl.BlockSpec((2 * n_heads, tm, LANES), lambda s: (0, tile(s), 0))
    own_slice = lambda s: jnp.minimum(s, n_sl - 1)
    late_slice = lambda s: jnp.clip(s - n_sl, 0, n_sl - 1)
    f32_slices = lambda w, which: pl.BlockSpec((None, w.shape[1] // n_sl, w.shape[2]),
                                               lambda s: (layer, which(s), 0))
    bf16_slices = [pl.BlockSpec((w.shape[1] // n_sl, w.shape[2]), lambda s: (late_slice(s), 0))
                   for w in late_weights]
    h1, xg, qv, kt, wbp, wba, wo, wgu2, wd2 = pl.pallas_call(
        functools.partial(_ffn_proj_kernel, pool_width=pool_width, sb_width=sb_width,
                          n_cast=len(late_weights), n_slices=n_sl),
        grid=(n_sl + n_tiles,),
        in_specs=[row_tile(d_model), _resident(n1.shape), _resident(nm.shape),
                  *[f32_slices(w, own_slice) for w in own_weights],
                  *[f32_slices(w, late_slice) for w in late_weights]],
        out_specs=[row_tile(d_model), row_tile(pool_width + gate_width), head_tiles,
                   pl.BlockSpec((n_pairs, 1, LANES, tm), lambda s: (0, tile(s), 0, 0)), *bf16_slices],
        out_shape=[jax.ShapeDtypeStruct((tokens, d_model), F32),
                   jax.ShapeDtypeStruct((tokens, pool_width + gate_width), F32),
                   jax.ShapeDtypeStruct((2 * n_heads, tokens, LANES), BF16),
                   jax.ShapeDtypeStruct((n_pairs, n_tiles, LANES, tm), BF16),
                   *[jax.ShapeDtypeStruct(w.shape[1:], BF16) for w in late_weights]],
        scratch_shapes=[pltpu.VMEM((tm, d_ff), BF16),
                        *[pltpu.VMEM(w.shape[1:], BF16) for w in own_weights]],
        compiler_params=params(dimension_semantics=("arbitrary",)),
        name="ffn1_proj",
    )(x2, n1, nm, *own_weights, *late_weights)

    q_tile = lambda width: pl.BlockSpec((tm, width), lambda b, i: (b * tiles_per_seq + i, 0))
    halo_blocks = tm // HALO
    out = pl.pallas_call(
        _mixer_ffn_kernel,
        grid=(batch, tiles_per_seq),
        in_specs=[q_tile(d_model),
                  pl.BlockSpec((2 * n_heads, tm, LANES), lambda b, i: (0, b * tiles_per_seq + i, 0)),
                  pl.BlockSpec((n_pairs, 1, LANES, tm), lambda b, i: (0, b * tiles_per_seq + i, 0, 0)),
                  q_tile(pool_width + gate_width),
                  pl.BlockSpec((HALO, pool_width),
                               lambda b, i: (jnp.maximum((b * tiles_per_seq + i) * halo_blocks - 1, 0), 0)),
                  _resident(pw.shape), _resident(ps.shape), _resident(wbp.shape), _resident(wba.shape),
                  _resident(wo.shape), _resident(n2.shape), _resident(wgu2.shape), _resident(wd2.shape),
                  _resident(nf.shape)],
        out_specs=q_tile(d_model),
        out_shape=jax.ShapeDtypeStruct((tokens, d_model), F32),
        scratch_shapes=[pltpu.VMEM((n_pairs, tiles_per_seq, LANES, tm), BF16),
                        pltpu.VMEM((n_heads, seq, LANES), BF16),
                        pltpu.VMEM((tm, sb_width), F32),
                        pltpu.VMEM((n_heads, tm, LANES), F32),
                        pltpu.VMEM((tm, pool_width), BF16),
                        pltpu.VMEM((tm, d_ff), BF16)],
        compiler_params=params(dimension_semantics=("arbitrary", "arbitrary")),
        name="mixer_ffn2",
    )(h1, qv, kt, xg, xg, pw, ps, wbp, wba, wo, n2, wgu2, wd2, nf)
    return out


def kernel(x, ffn1_norm, ffn1_w_gate_up, ffn1_w_down, mix_norm, w_in, pool_w_group, pool_scale,
           w_branch_pool, w_branch_attn, w_out, ffn2_norm, ffn2_w_gate_up, ffn2_w_down, final_norm):
    batch, seq, d_model = x.shape
    depth = ffn1_norm.shape[0]
    row = lambda a: a.reshape(1, -1)
    h = x.reshape(batch * seq, d_model)
    for l in range(depth):
        assert l == depth - 1
        h = _layer(h, batch, seq, l,
                   row(ffn1_norm[l]), (ffn1_w_gate_up, ffn1_w_down, w_in), row(mix_norm[l]),
                   pool_w_group[l].astype(BF16), row(pool_scale[l]),
                   (w_branch_pool, w_branch_attn, w_out, ffn2_w_gate_up, ffn2_w_down),
                   row(ffn2_norm[l]), row(final_norm))
    return h.reshape(batch, seq, d_model)
```
